```python
import math
import jax, jax.numpy as jnp
from jax import lax
import numpy as np


D_MODEL = 1024
BATCH = 8
SEQ = 4096
DEPTH = 4

CHUNK = 64
N_MIXERS = 3
EPS = 1e-6
D_FF = 4 * D_MODEL
MEM_LEN = 256
X_HEADS = 4
X_HEAD_DIM = 256
X_WIDTH = X_HEADS * X_HEAD_DIM
SSM_D_INNER = 2 * D_MODEL
SSM_HEAD_DIM = 64
SSM_HEADS = SSM_D_INNER // SSM_HEAD_DIM
SSM_GROUPS = 8
SSM_HPG = SSM_HEADS // SSM_GROUPS
SSM_STATE = 128
SSM_CONV = 4
SSM_BC = SSM_GROUPS * SSM_STATE
SSM_CONV_DIM = SSM_D_INNER + 2 * SSM_BC
SSM_IN = SSM_D_INNER + SSM_CONV_DIM + SSM_HEADS + X_WIDTH
DIFF_HEADS = 8
DIFF_HEAD_DIM = 64
DIFF_V_DIM = 2 * DIFF_HEAD_DIM
DIFF_QK = DIFF_HEADS * 2 * DIFF_HEAD_DIM
DIFF_WIDTH = DIFF_HEADS * DIFF_V_DIM
DIFF_IN = 2 * DIFF_QK + DIFF_WIDTH + X_WIDTH
Q_BLOCK = 128
SGU_BLOCK = 128
SGU_GROUPS = 8
SGU_WIDTH = D_MODEL
SGU_GROUP_DIM = SGU_WIDTH // SGU_GROUPS
SGU_IN = 2 * SGU_WIDTH + X_WIDTH
N_SSM = (DEPTH + 2) // 3
N_DIFF = (DEPTH + 1) // 3
N_SGU = DEPTH // 3

kernel_name = 'hybrid_streaming_encoder_trunk'


def rmsnorm(x, g):
    xf = x.astype(jnp.float32)
    y = xf * lax.rsqrt(jnp.mean(xf * xf, axis=-1, keepdims=True) + EPS)
    return (y * g.astype(jnp.float32)).astype(x.dtype)


def memory_cross_attention(q, mem_n, w_kv, gq, gk):
    b, l, _ = q.shape
    m = mem_n.shape[1]
    k, v = jnp.split(mem_n @ w_kv, 2, axis=-1)
    q = rmsnorm(q.reshape(b, l, X_HEADS, X_HEAD_DIM), gq)
    k = rmsnorm(k.reshape(b, m, X_HEADS, X_HEAD_DIM), gk)
    v = v.reshape(b, m, X_HEADS, X_HEAD_DIM)
    s = jnp.einsum('blhd,bmhd->bhlm', q, k).astype(jnp.float32) * (X_HEAD_DIM ** -0.5)
    p = jax.nn.softmax(s, axis=-1).astype(v.dtype)
    return jnp.einsum('bhlm,bmhd->blhd', p, v).reshape(b, l, X_WIDTH)


def causal_depthwise_conv(x, w, bias):
    y = lax.conv_general_dilated(
        x, w[:, None, :].astype(x.dtype), window_strides=(1,),
        padding=[(SSM_CONV - 1, 0)], dimension_numbers=('NWC', 'WIO', 'NWC'),
        feature_group_count=x.shape[-1])
    return y + bias


def segsum_exp(a):
    l = a.shape[-1]
    cs = jnp.cumsum(a, axis=-1)
    mask = jnp.tril(jnp.ones((l, l), dtype=bool))
    return jnp.exp(jnp.where(mask, cs[..., :, None] - cs[..., None, :], -jnp.inf))


def ssd_chunked(x, dt, a, bm, cm):
    f32 = jnp.float32
    b, l = x.shape[0], x.shape[1]
    nc = l // CHUNK
    x = x.astype(f32).reshape(b, nc, CHUNK, SSM_GROUPS, SSM_HPG, SSM_HEAD_DIM)
    dt = dt.astype(f32).reshape(b, nc, CHUNK, SSM_GROUPS, SSM_HPG)
    bm = bm.astype(f32).reshape(b, nc, CHUNK, SSM_GROUPS, SSM_STATE)
    cm = cm.astype(f32).reshape(b, nc, CHUNK, SSM_GROUPS, SSM_STATE)
    xdt = x * dt[..., None]
    adt = jnp.moveaxis(dt * a.astype(f32).reshape(SSM_GROUPS, SSM_HPG), 2, -1)
    a_cs = jnp.cumsum(adt, axis=-1)
    cb = jnp.einsum('bclgn,bcsgn->bcgls', cm, bm)
    y_diag = jnp.einsum('bcgls,bcgrls,bcsgrp->bclgrp', cb, segsum_exp(adt), xdt)
    decay_to_end = jnp.exp(a_cs[..., -1:] - a_cs)
    decay_from_start = jnp.exp(a_cs)
    chunk_decay = jnp.exp(a_cs[..., -1])

    def step(h, inp):
        b_c, c_c, xdt_c, dte_c, dfs_c, cd_c = inp
        y_off = jnp.einsum('blgn,bgrpn,bgrl->blgrp', c_c, h, dfs_c)
        h = h * cd_c[..., None, None] + jnp.einsum('blgn,bgrl,blgrp->bgrpn', b_c, dte_c, xdt_c)
        return h, y_off

    h0 = jnp.zeros((b, SSM_GROUPS, SSM_HPG, SSM_HEAD_DIM, SSM_STATE), f32)
    xs = (jnp.moveaxis(bm, 1, 0), jnp.moveaxis(cm, 1, 0), jnp.moveaxis(xdt, 1, 0),
          jnp.moveaxis(decay_to_end, 1, 0), jnp.moveaxis(decay_from_start, 1, 0),
          jnp.moveaxis(chunk_decay, 1, 0))
    _, y_off = lax.scan(step, h0, xs)
    y = y_diag + jnp.moveaxis(y_off, 0, 1)
    return y.reshape(b, l, SSM_HEADS, SSM_HEAD_DIM)


def mamba2_mixer(proj, conv_w, conv_b, dt_bias, a_log, d_skip, gate_norm):
    b, l, _ = proj.shape
    z, xbc, dt = jnp.split(proj, [SSM_D_INNER, SSM_D_INNER + SSM_CONV_DIM], axis=-1)
    xbc = jax.nn.silu(causal_depthwise_conv(xbc, conv_w, conv_b))
    xs, bm, cm = jnp.split(xbc, [SSM_D_INNER, SSM_D_INNER + SSM_BC], axis=-1)
    dt = jax.nn.softplus(dt.astype(jnp.float32) + dt_bias.astype(jnp.float32))
    a = -jnp.exp(a_log.astype(jnp.float32))
    xh = xs.reshape(b, l, SSM_HEADS, SSM_HEAD_DIM)
    y = ssd_chunked(xh, dt, a, bm.reshape(b, l, SSM_GROUPS, SSM_STATE),
                    cm.reshape(b, l, SSM_GROUPS, SSM_STATE))
    y = (y + d_skip.astype(jnp.float32)[:, None] * xh.astype(jnp.float32)).astype(proj.dtype)
    yg = (y.reshape(b, l, SSM_D_INNER) * jax.nn.silu(z)).reshape(b, l, SSM_GROUPS, -1)
    return rmsnorm(yg, gate_norm.reshape(SSM_GROUPS, -1)).reshape(b, l, SSM_D_INNER)


def alibi_slopes(n_heads):
    return jnp.exp2(-8.0 * jnp.arange(1, n_heads + 1, dtype=jnp.float32) / n_heads)


def differential_attention(proj, q_norm, k_norm, lq1, lk1, lq2, lk2, sub_norm, lambda_init):
    f32 = jnp.float32
    b, l, _ = proj.shape
    q, k, v = jnp.split(proj, [DIFF_QK, 2 * DIFF_QK], axis=-1)
    q = rmsnorm(q.reshape(b, l, DIFF_HEADS, 2, DIFF_HEAD_DIM), q_norm)
    k = rmsnorm(k.reshape(b, l, DIFF_HEADS, 2, DIFF_HEAD_DIM), k_norm)
    v = v.reshape(b, l, DIFF_HEADS, DIFF_V_DIM)
    lam = (jnp.exp(jnp.sum(lq1.astype(f32) * lk1.astype(f32)))
           - jnp.exp(jnp.sum(lq2.astype(f32) * lk2.astype(f32))) + lambda_init)
    slopes = alibi_slopes(DIFF_HEADS)
    scale = DIFF_HEAD_DIM ** -0.5
    pos = jnp.arange(l)
    outs = []
    for qb in range(l // Q_BLOCK):
        q0, q1 = qb * Q_BLOCK, (qb + 1) * Q_BLOCK
        tq, tk = pos[q0:q1], pos[:q1]
        allowed = (tk[None, :] // CHUNK) <= (tq[:, None] // CHUNK)
        dist = jnp.abs(tq[:, None] - tk[None, :]).astype(f32)
        bias = jnp.where(allowed[None], -slopes[:, None, None] * dist[None], -jnp.inf)
        s = jnp.einsum('bqhmd,bkhmd->bhmqk', q[:, q0:q1], k[:, :q1]).astype(f32) * scale
        p = jax.nn.softmax(s + bias[:, None], axis=-1)
        attn = (p[:, :, 0] - lam * p[:, :, 1]).astype(v.dtype)
        outs.append(jnp.einsum('bhqk,bkhe->bqhe', attn, v[:, :q1]))
    o = rmsnorm(jnp.concatenate(outs, axis=1), sub_norm) * (1.0 - lambda_init)
    return o.reshape(b, l, DIFF_WIDTH)


def spatial_gating_mixer(proj, v_norm, w_s, b_s):
    b, l, _ = proj.shape
    u, v = jnp.split(jax.nn.gelu(proj, approximate=False), 2, axis=-1)
    v = rmsnorm(v.reshape(b, l, SGU_GROUPS, SGU_GROUP_DIM), v_norm.reshape(SGU_GROUPS, SGU_GROUP_DIM))
    v = v.reshape(b, l // SGU_BLOCK, SGU_BLOCK, SGU_GROUPS, SGU_GROUP_DIM)
    pos = jnp.arange(SGU_BLOCK)
    mask = (pos[None, :] // CHUNK) <= (pos[:, None] // CHUNK)
    ws = jnp.where(mask[None], w_s, 0.0).astype(v.dtype)
    mixed = jnp.einsum('gts,bnsgc->bntgc', ws, v) + b_s.T[:, :, None]
    return u * mixed.reshape(b, l, SGU_WIDTH)


def squared_relu_mlp(x, w1, w2):
    return jnp.square(jax.nn.relu(x @ w1)) @ w2


def setup_inputs(seed: int = 0) -> dict:
    key = jax.random.key(seed)
    ks = iter(jax.random.split(key, 48))
    f32 = jnp.float32

    def dense(shape, fan_in):
        return jax.random.normal(next(ks), shape, f32) * fan_in ** -0.5

    def gain(shape):
        return 1.0 + 0.02 * jax.random.normal(next(ks), shape, f32)

    def small(shape, s=0.02):
        return s * jax.random.normal(next(ks), shape, f32)

    dt0 = jnp.exp(jax.random.uniform(next(ks), (N_SSM, SSM_HEADS), f32,
                                     math.log(1e-3), math.log(1e-1)))
    return {
        'x': jax.random.normal(next(ks), (BATCH, SEQ, D_MODEL), f32),
        'mem': jax.random.normal(next(ks), (BATCH, MEM_LEN, D_MODEL), f32),
        'norm_mix': gain((DEPTH, D_MODEL)),
        'norm_ffn': gain((DEPTH, D_MODEL)),
        'mem_norm': gain((D_MODEL,)),
        'w_mem_kv': dense((DEPTH, D_MODEL, 2 * X_WIDTH), D_MODEL),
        'xq_norm': gain((DEPTH, X_HEAD_DIM)),
        'xk_norm': gain((DEPTH, X_HEAD_DIM)),
        'w_ff1': dense((DEPTH, D_MODEL, D_FF), D_MODEL),
        'w_ff2': dense((DEPTH, D_FF, D_MODEL), D_FF),
        'w_in_ssm': dense((N_SSM, D_MODEL, SSM_IN), D_MODEL),
        'ssm_conv_w': dense((N_SSM, SSM_CONV, SSM_CONV_DIM), SSM_CONV),
        'ssm_conv_b': small((N_SSM, SSM_CONV_DIM)),
        'ssm_dt_bias': dt0 + jnp.log(-jnp.expm1(-dt0)),
        'ssm_a_log': jnp.log(jax.random.uniform(next(ks), (N_SSM, SSM_HEADS), f32, 1.0, 16.0)),
        'ssm_d': gain((N_SSM, SSM_HEADS)),
        'ssm_gate_norm': gain((N_SSM, SSM_D_INNER)),
        'w_out_ssm': dense((N_SSM, SSM_D_INNER + X_WIDTH, D_MODEL), SSM_D_INNER + X_WIDTH),
        'w_in_diff': dense((N_DIFF, D_MODEL, DIFF_IN), D_MODEL),
        'diff_q_norm': gain((N_DIFF, DIFF_HEAD_DIM)),
        'diff_k_norm': gain((N_DIFF, DIFF_HEAD_DIM)),
        'diff_lq1': small((N_DIFF, DIFF_HEAD_DIM), 0.1),
        'diff_lk1': small((N_DIFF, DIFF_HEAD_DIM), 0.1),
        'diff_lq2': small((N_DIFF, DIFF_HEAD_DIM), 0.1),
        'diff_lk2': small((N_DIFF, DIFF_HEAD_DIM), 0.1),
        'diff_sub_norm': gain((N_DIFF, DIFF_V_DIM)),
        'w_out_diff': dense((N_DIFF, DIFF_WIDTH + X_WIDTH, D_MODEL), DIFF_WIDTH + X_WIDTH),
        'w_in_sgu': dense((N_SGU, D_MODEL, SGU_IN), D_MODEL),
        'sgu_v_norm': gain((N_SGU, SGU_WIDTH)),
        'sgu_w_s': dense((N_SGU, SGU_GROUPS, SGU_BLOCK, SGU_BLOCK), SGU_BLOCK),
        'sgu_b_s': gain((N_SGU, SGU_GROUPS, SGU_BLOCK)),
        'w_out_sgu': dense((N_SGU, SGU_WIDTH + X_WIDTH, D_MODEL), SGU_WIDTH + X_WIDTH),
    }


def reference(x, mem, norm_mix, norm_ffn, mem_norm, w_mem_kv, xq_norm, xk_norm, w_ff1, w_ff2,
              w_in_ssm, ssm_conv_w, ssm_conv_b, ssm_dt_bias, ssm_a_log, ssm_d, ssm_gate_norm, w_out_ssm,
              w_in_diff, diff_q_norm, diff_k_norm, diff_lq1, diff_lk1, diff_lq2, diff_lk2, diff_sub_norm,
              w_out_diff, w_in_sgu, sgu_v_norm, sgu_w_s, sgu_b_s, w_out_sgu):
    mem_n = rmsnorm(mem, mem_norm)
    for i in range(DEPTH):
        kind, j = i % N_MIXERS, i // N_MIXERS
        h = rmsnorm(x, norm_mix[i])
        if kind == 0:
            proj = h @ w_in_ssm[j]
            mix_in, xq = proj[..., :-X_WIDTH], proj[..., -X_WIDTH:]
            mix_out = mamba2_mixer(mix_in, ssm_conv_w[j], ssm_conv_b[j], ssm_dt_bias[j],
                                   ssm_a_log[j], ssm_d[j], ssm_gate_norm[j])
            w_out = w_out_ssm[j]
        elif kind == 1:
            proj = h @ w_in_diff[j]
            mix_in, xq = proj[..., :-X_WIDTH], proj[..., -X_WIDTH:]
            lambda_init = 0.8 - 0.6 * math.exp(-0.3 * i)
            mix_out = differential_attention(mix_in, diff_q_norm[j], diff_k_norm[j], diff_lq1[j],
                                             diff_lk1[j], diff_lq2[j], diff_lk2[j],
                                             diff_sub_norm[j], lambda_init)
            w_out = w_out_diff[j]
        else:
            proj = h @ w_in_sgu[j]
            mix_in, xq = proj[..., :-X_WIDTH], proj[..., -X_WIDTH:]
            mix_out = spatial_gating_mixer(mix_in, sgu_v_norm[j], sgu_w_s[j], sgu_b_s[j])
            w_out = w_out_sgu[j]
        cross = memory_cross_attention(xq, mem_n, w_mem_kv[i], xq_norm[i], xk_norm[i])
        x = x + jnp.concatenate([mix_out, cross], axis=-1) @ w_out
        x = x + squared_relu_mlp(rmsnorm(x, norm_ffn[i]), w_ff1[i], w_ff2[i])
    return x
```

```python
import functools
import math

import jax
import jax.numpy as jnp
from jax import lax
from jax.experimental import pallas as pl
from jax.experimental.pallas import tpu as pltpu

F32 = jnp.float32
BF16 = jnp.bfloat16

EPS = 1e-6
CHUNK = 64
N_MIXERS = 3
LANES = 128
VMEM_LIMIT = 56 * 1024 * 1024

X_HEADS = 4
X_HEAD_DIM = 256
X_WIDTH = X_HEADS * X_HEAD_DIM
SSM_HEAD_DIM = 64
SSM_HEADS = 32
SSM_GROUPS = 8
SSM_HPG = SSM_HEADS // SSM_GROUPS
SSM_STATE = 128
SSM_CONV = 4
SSM_D_INNER = SSM_HEADS * SSM_HEAD_DIM
SSM_BC = SSM_GROUPS * SSM_STATE
SSM_TILE = 256
DIFF_HEADS = 8
DIFF_HEAD_DIM = 64
DIFF_V_DIM = 2 * DIFF_HEAD_DIM
DIFF_TILE = 256
SGU_BLOCK = 128
SGU_GROUPS = 8
FF_CHUNK = 1024


def _params(*sem):
    return pltpu.CompilerParams(dimension_semantics=sem, vmem_limit_bytes=VMEM_LIMIT)


def _rms(x, g, n=None):
    ms = jnp.mean(x * x, axis=-1, keepdims=True)
    return x * lax.rsqrt(ms + EPS) * g


def _resident(shape):
    nd = len(shape)
    return pl.BlockSpec(shape, lambda *_: (0,) * nd, pipeline_mode=pl.Buffered(1))


def _norm_matmul_body(x_ref, g_ref, w_ref, o_ref, h_ref):
    @pl.when(pl.program_id(1) == 0)
    def _():
        h_ref[...] = _rms(x_ref[...], g_ref[...]).astype(BF16)

    o_ref[...] = jnp.dot(h_ref[...], w_ref[...], preferred_element_type=F32).astype(o_ref.dtype)


def norm_matmul(x, g, w, *, tm, tn, out_dtype):
    t, d = x.shape
    n = w.shape[1]
    assert t % tm == 0 and n % tn == 0
    return pl.pallas_call(
        _norm_matmul_body,
        grid=(t // tm, n // tn),
        in_specs=[pl.BlockSpec((tm, d), lambda i, j: (i, 0)),
                  pl.BlockSpec((1, d), lambda i, j: (0, 0)),
                  pl.BlockSpec((d, tn), lambda i, j: (0, j))],
        out_specs=pl.BlockSpec((tm, tn), lambda i, j: (i, j)),
        out_shape=jax.ShapeDtypeStruct((t, n), out_dtype),
        scratch_shapes=[pltpu.VMEM((tm, d), BF16)],
        compiler_params=_params("parallel", "arbitrary"),
        name="norm_matmul",
    )(x, g.reshape(1, d), w)


def _cross_body(q_ref, kv_ref, gq_ref, gk_ref, o_ref, kn_ref, vb_ref):
    hd = X_HEAD_DIM

    @pl.when(pl.program_id(1) == 0)
    def _():
        for h in range(X_HEADS):
            k = kv_ref[:, h * hd:(h + 1) * hd].astype(F32)
            kn_ref[:, h * hd:(h + 1) * hd] = _rms(k, gk_ref[...]).astype(BF16)
        vb_ref[...] = kv_ref[:, X_WIDTH:].astype(BF16)

    for h in range(X_HEADS):
        sl = slice(h * hd, (h + 1) * hd)
        q = _rms(q_ref[:, sl].astype(F32), gq_ref[...]) * (hd ** -0.5)
        s = lax.dot_general(q.astype(BF16), kn_ref[:, sl], (((1,), (1,)), ((), ())),
                            preferred_element_type=F32)
        p = jnp.exp(s - jnp.max(s, axis=-1, keepdims=True))
        l = jnp.sum(p, axis=-1, keepdims=True)
        o = jnp.dot(p.astype(BF16), vb_ref[:, sl], preferred_element_type=F32)
        o_ref[:, sl] = (o / l).astype(o_ref.dtype)


def cross_attention(proj, q_col, kv_all, layer, gq, gk, *, batch, tq, out_dtype):
    t = proj.shape[0]
    seq = t // batch
    m = kv_all.shape[0] // batch
    nq = seq // tq
    return pl.pallas_call(
        _cross_body,
        grid=(batch, nq),
        in_specs=[pl.BlockSpec((tq, X_WIDTH), lambda b, i: (b * nq + i, q_col)),
                  pl.BlockSpec((m, 2 * X_WIDTH), lambda b, i: (b, layer)),
                  pl.BlockSpec((1, X_HEAD_DIM), lambda b, i: (0, 0)),
                  pl.BlockSpec((1, X_HEAD_DIM), lambda b, i: (0, 0))],
        out_specs=pl.BlockSpec((tq, X_WIDTH), lambda b, i: (b * nq + i, 0)),
        out_shape=jax.ShapeDtypeStruct((t, X_WIDTH), out_dtype),
        scratch_shapes=[pltpu.VMEM((m, X_WIDTH), BF16), pltpu.VMEM((m, X_WIDTH), BF16)],
        compiler_params=_params("parallel", "arbitrary"),
        name="cross_attention",
    )(proj, kv_all, gq.reshape(1, -1), gk.reshape(1, -1))


def _out_proj_body(x_ref, mix_ref, cross_ref, wa_ref, wb_ref, o_ref):
    acc = x_ref[...]
    acc = acc + jnp.dot(mix_ref[...].astype(BF16), wa_ref[...], preferred_element_type=F32)
    acc = acc + jnp.dot(cross_ref[...].astype(BF16), wb_ref[...], preferred_element_type=F32)
    o_ref[...] = acc


def out_proj(x, mix, cross, wa, wb, *, tm):
    t, d = x.shape
    wm, wc = mix.shape[1], cross.shape[1]
    return pl.pallas_call(
        _out_proj_body,
        grid=(t // tm,),
        in_specs=[pl.BlockSpec((tm, d), lambda i: (i, 0)),
                  pl.BlockSpec((tm, wm), lambda i: (i, 0)),
                  pl.BlockSpec((tm, wc), lambda i: (i, 0)),
                  _resident((wm, d)),
                  _resident((wc, d))],
        out_specs=pl.BlockSpec((tm, d), lambda i: (i, 0)),
        out_shape=jax.ShapeDtypeStruct((t, d), F32),
        compiler_params=_params("parallel"),
        name="out_proj",
    )(x, mix, cross, wa, wb)


def _ffn_body(x_ref, g_ref, w1_ref, w2_ref, o_ref):
    x = x_ref[...]
    h = _rms(x, g_ref[...]).astype(BF16)
    acc = x
    for c in range(w1_ref.shape[1] // FF_CHUNK):
        sl = slice(c * FF_CHUNK, (c + 1) * FF_CHUNK)
        a = jnp.maximum(jnp.dot(h, w1_ref[:, sl], preferred_element_type=F32), 0.0)
        acc = acc + jnp.dot((a * a).astype(BF16), w2_ref[sl, :], preferred_element_type=F32)
    o_ref[...] = acc


def ffn(x, g, w1, w2, *, tm):
    t, d = x.shape
    f = w1.shape[1]
    assert f % FF_CHUNK == 0
    return pl.pallas_call(
        _ffn_body,
        grid=(t // tm,),
        in_specs=[pl.BlockSpec((tm, d), lambda i: (i, 0)),
                  _resident((1, d)),
                  _resident((d, f)),
                  _resident((f, d))],
        out_specs=pl.BlockSpec((tm, d), lambda i: (i, 0)),
        out_shape=jax.ShapeDtypeStruct((t, d), F32),
        compiler_params=_params("parallel"),
        name="ffn",
    )(x, g.reshape(1, d), w1, w2)


def _gelu(x):
    return 0.5 * x * (1.0 + lax.erf(x * (2.0 ** -0.5)))


def _sgu_body(u_ref, v_ref, vn_ref, ws_ref, bs_ref, o_ref):
    tm = u_ref.shape[0]
    blk = SGU_BLOCK
    gd = v_ref.shape[1] // SGU_GROUPS
    row = lax.broadcasted_iota(jnp.int32, (blk, blk), 0)
    col = lax.broadcasted_iota(jnp.int32, (blk, blk), 1)
    allowed = (col // CHUNK) <= (row // CHUNK)
    for g in range(SGU_GROUPS):
        sl = slice(g * gd, (g + 1) * gd)
        vg = _rms(_gelu(v_ref[:, sl].astype(F32)), vn_ref[:, sl]).astype(BF16)
        ws = jnp.where(allowed, ws_ref[g], 0.0).astype(BF16)
        bias = bs_ref[:, g:g + 1]
        for n in range(tm // blk):
            rs = slice(n * blk, (n + 1) * blk)
            mixed = jnp.dot(ws, vg[rs, :], preferred_element_type=F32) + bias
            o_ref[rs, sl] = (_gelu(u_ref[rs, sl].astype(F32)) * mixed).astype(o_ref.dtype)


def sgu_mixer(proj, v_norm, w_s, b_s, *, tm, out_dtype):
    t = proj.shape[0]
    w = v_norm.shape[0]
    return pl.pallas_call(
        _sgu_body,
        grid=(t // tm,),
        in_specs=[pl.BlockSpec((tm, w), lambda i: (i, 0)),
                  pl.BlockSpec((tm, w), lambda i: (i, 1)),
                  _resident((1, w)),
                  _resident((SGU_GROUPS, SGU_BLOCK, SGU_BLOCK)),
                  _resident((SGU_BLOCK, SGU_GROUPS))],
        out_specs=pl.BlockSpec((tm, w), lambda i: (i, 0)),
        out_shape=jax.ShapeDtypeStruct((t, w), out_dtype),
        compiler_params=_params("parallel"),
        name="sgu_mixer",
    )(proj, proj, v_norm.reshape(1, w), w_s, b_s.T)


def _half_rms(x, g, first_half):
    sq = x * x
    s1 = jnp.sum(jnp.where(first_half, sq, 0.0), axis=-1, keepdims=True)
    s2 = jnp.sum(jnp.where(first_half, 0.0, sq), axis=-1, keepdims=True)
    inv = 1.0 / DIFF_HEAD_DIM
    r = jnp.where(first_half, lax.rsqrt(s1 * inv + EPS), lax.rsqrt(s2 * inv + EPS))
    return x * r * g


def _diff_attn_body(slopes_ref, q_ref, k_ref, v_ref, qg_ref, kg_ref, lq1_ref, lk1_ref, lq2_ref,
                    lk2_ref, sub_ref, o_ref, kn_ref, vb_ref, m_ref, l_ref, acc_ref, *, lambda_init):
    tq = q_ref.shape[0]
    tk = tq
    h = pl.program_id(1)
    qi = pl.program_id(2)
    slope = slopes_ref[h]
    first_half = lax.broadcasted_iota(jnp.int32, (1, LANES), 1) < DIFF_HEAD_DIM

    @pl.when(qi == 0)
    def _():
        kn_ref[...] = _half_rms(k_ref[...].astype(F32), kg_ref[...], first_half).astype(BF16)
        vb_ref[...] = v_ref[...].astype(BF16)

    qn = _half_rms(q_ref[...].astype(F32), qg_ref[...], first_half) * (DIFF_HEAD_DIM ** -0.5)
    qm = (jnp.where(first_half, qn, 0.0).astype(BF16), jnp.where(first_half, 0.0, qn).astype(BF16))

    m_ref[...] = jnp.full(m_ref.shape, -jnp.inf, F32)
    l_ref[...] = jnp.zeros(l_ref.shape, F32)
    acc_ref[...] = jnp.zeros(acc_ref.shape, F32)

    def update(j, bias):
        kb = kn_ref[pl.ds(pl.multiple_of(j * tk, tk), tk), :]
        vb = vb_ref[pl.ds(pl.multiple_of(j * tk, tk), tk), :]
        for mp in range(2):
            s = lax.dot_general(qm[mp], kb, (((1,), (1,)), ((), ())), preferred_element_type=F32) + bias
            m_old = m_ref[mp]
            m_new = jnp.maximum(m_old, jnp.max(s, axis=-1, keepdims=True))
            p = jnp.exp(s - m_new)
            alpha = jnp.exp(m_old - m_new)
            l_ref[mp] = alpha * l_ref[mp] + jnp.sum(p, axis=-1, keepdims=True)
            acc_ref[mp] = alpha * acc_ref[mp] + jnp.dot(p.astype(BF16), vb, preferred_element_type=F32)
            m_ref[mp] = m_new

    col = lax.broadcasted_iota(jnp.int32, (1, tk), 1)

    def past_tile(j, carry):
        update(j, slope * ((j - qi) * tk + col).astype(F32))
        return carry

    lax.fori_loop(0, qi, past_tile, 0)

    r = lax.broadcasted_iota(jnp.int32, (tq, tk), 0)
    c = lax.broadcasted_iota(jnp.int32, (tq, tk), 1)
    dist = jnp.abs(r - c).astype(F32)
    diag_bias = jnp.where((c // CHUNK) <= (r // CHUNK), slope * (r.astype(F32) - dist), -jnp.inf)
    update(qi, diag_bias)

    lam = (jnp.exp(jnp.sum(lq1_ref[...] * lk1_ref[...], axis=-1, keepdims=True))
           - jnp.exp(jnp.sum(lq2_ref[...] * lk2_ref[...], axis=-1, keepdims=True)) + lambda_init)
    o = acc_ref[0] / l_ref[0] - lam * (acc_ref[1] / l_ref[1])
    o_ref[...] = (_rms(o, sub_ref[...]) * (1.0 - lambda_init)).astype(o_ref.dtype)


def diff_attention(proj, q_norm, k_norm, lq1, lk1, lq2, lk2, sub_norm, lambda_init, *, batch,
                   out_dtype):
    t = proj.shape[0]
    seq = t // batch
    tq = min(DIFF_TILE, seq)
    nq = seq // tq
    hw = DIFF_V_DIM
    slopes = jnp.exp2(-8.0 * jnp.arange(1, DIFF_HEADS + 1, dtype=F32) / DIFF_HEADS)
    vec = lambda a: a.reshape(1, -1).astype(F32)
    two = lambda a: jnp.tile(a.reshape(1, -1).astype(F32), (1, 2))
    small = lambda n: pl.BlockSpec((1, n), lambda b, h, i: (0, 0))
    return pl.pallas_call(
        functools.partial(_diff_attn_body, lambda_init=lambda_init),
        grid=(batch, DIFF_HEADS, nq),
        in_specs=[pl.BlockSpec(memory_space=pltpu.SMEM),
                  pl.BlockSpec((tq, hw), lambda b, h, i: (b * nq + i, h)),
                  pl.BlockSpec((seq, hw), lambda b, h, i: (b, DIFF_HEADS + h)),
                  pl.BlockSpec((seq, hw), lambda b, h, i: (b, 2 * DIFF_HEADS + h)),
                  small(hw), small(hw),
                  small(DIFF_HEAD_DIM), small(DIFF_HEAD_DIM), small(DIFF_HEAD_DIM), small(DIFF_HEAD_DIM),
                  small(hw)],
        out_specs=pl.BlockSpec((tq, hw), lambda b, h, i: (b * nq + i, h)),
        out_shape=jax.ShapeDtypeStruct((t, DIFF_HEADS * hw), out_dtype),
        scratch_shapes=[pltpu.VMEM((seq, hw), BF16), pltpu.VMEM((seq, hw), BF16),
                        pltpu.VMEM((2, tq, 1), F32), pltpu.VMEM((2, tq, 1), F32),
                        pltpu.VMEM((2, tq, hw), F32)],
        compiler_params=_params("parallel", "parallel", "arbitrary"),
        name="diff_attention",
    )(slopes, proj, proj, proj, two(q_norm), two(k_norm), vec(lq1), vec(lk1), vec(lq2), vec(lk2),
      vec(sub_norm))


def _silu(x):
    return x * (1.0 / (1.0 + jnp.exp(-x)))


def _softplus(x):
    return jnp.maximum(x, 0.0) + jnp.log1p(jnp.exp(-jnp.abs(x)))


def _pair_expand(a, h, first_half):
    return jnp.where(first_half, a[:, h:h + 1], a[:, h + 1:h + 2])


def _ssm_body(z_ref, xs_ref, bc_ref, dt_ref, cwx_ref, cwb_ref, cbx_ref, cbb_ref, dtb_ref, alog_ref,
              dskip_ref, gn_ref, o_ref, xcat_ref, bcat_ref, state_ref, xc_ref, bcc_ref, e_ref):
    q = z_ref.shape[0]
    pad = 8
    first_half = lax.broadcasted_iota(jnp.int32, (1, LANES), 1) < SSM_HEAD_DIM

    @pl.when(pl.program_id(1) == 0)
    def _():
        xcat_ref[0:pad, :] = jnp.zeros((pad, xcat_ref.shape[1]), F32)
        bcat_ref[0:pad, :] = jnp.zeros((pad, bcat_ref.shape[1]), F32)
        state_ref[...] = jnp.zeros(state_ref.shape, F32)

    def conv_silu(src_ref, cat_ref, w_ref, b_ref, dst_ref):
        cat_ref[pad:pad + q, :] = src_ref[...].astype(F32)
        acc = b_ref[...] + w_ref[0:1, :] * cat_ref[pad - 3:pad - 3 + q, :]
        for k in range(1, SSM_CONV):
            acc = acc + w_ref[k:k + 1, :] * cat_ref[pad - 3 + k:pad - 3 + k + q, :]
        dst_ref[...] = _silu(acc)
        cat_ref[0:pad, :] = cat_ref[q:q + pad, :]

    conv_silu(xs_ref, xcat_ref, cwx_ref, cbx_ref, xc_ref)
    conv_silu(bc_ref, bcat_ref, cwb_ref, cbb_ref, bcc_ref)

    dtv = _softplus(dt_ref[...] + dtb_ref[...])
    adt = dtv * (-jnp.exp(alog_ref[...]))
    row = lax.broadcasted_iota(jnp.int32, (q, q), 0)
    col = lax.broadcasted_iota(jnp.int32, (q, q), 1)
    causal = col <= row
    cs = jnp.dot(causal.astype(F32), adt, preferred_element_type=F32, precision=lax.Precision.HIGHEST)
    cs_t = cs.T
    cs_last = cs[q - 1:q, :]
    dfs = jnp.exp(cs)
    dte = jnp.exp(cs_last - cs)
    cd = jnp.exp(cs_last)

    for pr in range(SSM_HEADS // 2):
        h = 2 * pr
        e_ref[0, :, pr * LANES:(pr + 1) * LANES] = _pair_expand(dtv, h, first_half)
        e_ref[1, :, pr * LANES:(pr + 1) * LANES] = _pair_expand(dfs, h, first_half)
        e_ref[2, :, pr * LANES:(pr + 1) * LANES] = _pair_expand(dte, h, first_half)
        e_ref[3, 0:1, pr * LANES:(pr + 1) * LANES] = _pair_expand(cd, h, first_half)

    gw = SSM_HPG * SSM_HEAD_DIM
    for g in range(SSM_GROUPS):
        gs = slice(g * gw, (g + 1) * gw)
        bg = bcc_ref[:, g * SSM_STATE:(g + 1) * SSM_STATE].astype(BF16)
        cg = bcc_ref[:, SSM_BC + g * SSM_STATE:SSM_BC + (g + 1) * SSM_STATE].astype(BF16)
        cb = lax.dot_general(cg, bg, (((1,), (1,)), ((), ())), preferred_element_type=F32)
        xg = xc_ref[:, gs]
        xdt = xg * e_ref[0, :, gs]
        y = jnp.dot(cg, state_ref[g].astype(BF16), preferred_element_type=F32) * e_ref[1, :, gs]
        y = y + dskip_ref[:, gs] * xg
        yd = []
        for p2 in range(SSM_HPG // 2):
            xp = xdt[:, p2 * LANES:(p2 + 1) * LANES]
            acc = None
            for half in range(2):
                h = g * SSM_HPG + 2 * p2 + half
                decay = jnp.exp(jnp.where(causal, cs[:, h:h + 1] - cs_t[h:h + 1, :], -jnp.inf))
                m = (cb * decay).astype(BF16)
                keep = first_half if half == 0 else jnp.logical_not(first_half)
                part = jnp.dot(m, jnp.where(keep, xp, 0.0).astype(BF16), preferred_element_type=F32)
                acc = part if acc is None else acc + part
            yd.append(acc)
        y = y + jnp.concatenate(yd, axis=1)
        wgt = (xdt * e_ref[2, :, gs]).astype(BF16)
        state_ref[g] = state_ref[g] * e_ref[3, 0:1, gs] + lax.dot_general(
            bg, wgt, (((0,), (0,)), ((), ())), preferred_element_type=F32)
        yg = y * _silu(z_ref[:, gs].astype(F32))
        o_ref[:, gs] = _rms(yg, gn_ref[:, gs]).astype(o_ref.dtype)


def ssm_mixer(proj, dt_raw, conv_w, conv_b, dt_bias, a_log, d_skip, gate_norm, *, batch, out_dtype):
    t = proj.shape[0]
    seq = t // batch
    q = min(SSM_TILE, seq)
    nt = seq // q
    di = SSM_D_INNER
    pad_lanes = lambda a: jnp.pad(a.reshape(1, -1).astype(F32), ((0, 0), (0, LANES - a.shape[-1])))
    tile = lambda c: pl.BlockSpec((q, di), lambda b, i: (b * nt + i, c))
    return pl.pallas_call(
        _ssm_body,
        grid=(batch, nt),
        in_specs=[tile(0), tile(1), tile(2),
                  pl.BlockSpec((q, LANES), lambda b, i: (b * nt + i, 0)),
                  _resident((SSM_CONV, di)), _resident((SSM_CONV, di)),
                  _resident((1, di)), _resident((1, di)),
                  _resident((1, LANES)), _resident((1, LANES)),
                  _resident((1, di)), _resident((1, di))],
        out_specs=pl.BlockSpec((q, di), lambda b, i: (b * nt + i, 0)),
        out_shape=jax.ShapeDtypeStruct((t, di), out_dtype),
        scratch_shapes=[pltpu.VMEM((q + 8, di), F32), pltpu.VMEM((q + 8, di), F32),
                        pltpu.VMEM((SSM_GROUPS, SSM_STATE, SSM_HPG * SSM_HEAD_DIM), F32),
                        pltpu.VMEM((q, di), F32), pltpu.VMEM((q, di), F32),
                        pltpu.VMEM((4, q, di), F32)],
        compiler_params=_params("parallel", "arbitrary"),
        name="ssm_mixer",
    )(proj, proj, proj, dt_raw,
      conv_w[:, :di], conv_w[:, di:], conv_b[:di].reshape(1, di), conv_b[di:].reshape(1, di),
      pad_lanes(dt_bias), pad_lanes(a_log),
      jnp.repeat(d_skip.astype(F32), SSM_HEAD_DIM).reshape(1, di), gate_norm.reshape(1, di))


def kernel(x, mem, norm_mix, norm_ffn, mem_norm, w_mem_kv, xq_norm, xk_norm, w_ff1, w_ff2, w_in_ssm, ssm_conv_w, ssm_conv_b, ssm_dt_bias, ssm_a_log, ssm_d, ssm_gate_norm, w_out_ssm, w_in_diff, diff_q_norm, diff_k_norm, diff_lq1, diff_lk1, diff_lq2, diff_lk2, diff_sub_norm, w_out_diff, w_in_sgu, sgu_v_norm, sgu_w_s, sgu_b_s, w_out_sgu):
    batch, seq, d = x.shape
    depth = norm_mix.shape[0]
    t = batch * seq
    act = F32
    tm = min(512, t)
    xf = x.reshape(t, d)

    mem_f = mem.reshape(-1, d)
    w_kv_all = jnp.moveaxis(w_mem_kv, 0, 1).reshape(d, -1).astype(BF16)
    kv_all = norm_matmul(mem_f, mem_norm, w_kv_all, tm=min(512, mem_f.shape[0]), tn=2 * X_WIDTH,
                         out_dtype=act)

    for i in range(depth):
        kind, j = i % N_MIXERS, i // N_MIXERS
        if kind == 0:
            w_in = w_in_ssm[j]
            n_main = 2 * SSM_D_INNER + 2 * SSM_BC
            w_main = jnp.concatenate([w_in[:, :n_main], w_in[:, -X_WIDTH:]], axis=1).astype(BF16)
            w_dt = jnp.pad(w_in[:, n_main:n_main + SSM_HEADS], ((0, 0), (0, LANES - SSM_HEADS))).astype(BF16)
            proj = norm_matmul(xf, norm_mix[i], w_main, tm=tm, tn=1792, out_dtype=act)
            dt_raw = norm_matmul(xf, norm_mix[i], w_dt, tm=tm, tn=LANES, out_dtype=F32)
            mix = ssm_mixer(proj, dt_raw, ssm_conv_w[j], ssm_conv_b[j], ssm_dt_bias[j], ssm_a_log[j],
                            ssm_d[j], ssm_gate_norm[j], batch=batch, out_dtype=act)
            q_col = n_main // X_WIDTH
            w_out = w_out_ssm[j]
        elif kind == 1:
            proj = norm_matmul(xf, norm_mix[i], w_in_diff[j].astype(BF16), tm=tm, tn=1024, out_dtype=act)
            lambda_init = 0.8 - 0.6 * math.exp(-0.3 * i)
            mix = diff_attention(proj, diff_q_norm[j], diff_k_norm[j], diff_lq1[j], diff_lk1[j],
                                 diff_lq2[j], diff_lk2[j], diff_sub_norm[j], lambda_init, batch=batch,
                                 out_dtype=act)
            q_col = 3
            w_out = w_out_diff[j]
        else:
            proj = norm_matmul(xf, norm_mix[i], w_in_sgu[j].astype(BF16), tm=tm, tn=1024, out_dtype=act)
            mix = sgu_mixer(proj, sgu_v_norm[j], sgu_w_s[j], sgu_b_s[j], tm=tm, out_dtype=act)
            q_col = 2
            w_out = w_out_sgu[j]
        cross = cross_attention(proj, q_col, kv_all, i, xq_norm[i], xk_norm[i], batch=batch,
                                tq=min(512, seq), out_dtype=act)
        wm = mix.shape[1]
        xf = out_proj(xf, mix, cross, w_out[:wm].astype(BF16), w_out[wm:].astype(BF16), tm=tm)
        xf = ffn(xf, norm_ffn[i], w_ff1[i].astype(BF16), w_ff2[i].astype(BF16), tm=tm)
    return xf.reshape(batch, seq, d)
```

```python
import functools
import math

import jax
import jax.numpy as jnp
from jax import lax
from jax.experimental import pallas as pl
from jax.experimental.pallas import tpu as pltpu

F32 = jnp.float32
BF16 = jnp.bfloat16

EPS = 1e-6
CHUNK = 64
N_MIXERS = 3
LANES = 128
VMEM_LIMIT = 56 * 1024 * 1024

X_HEADS = 4
X_HEAD_DIM = 256
X_WIDTH = X_HEADS * X_HEAD_DIM
SSM_HEAD_DIM = 64
SSM_HEADS = 32
SSM_GROUPS = 8
SSM_HPG = SSM_HEADS // SSM_GROUPS
SSM_STATE = 128
SSM_CONV = 4
SSM_D_INNER = SSM_HEADS * SSM_HEAD_DIM
SSM_BC = SSM_GROUPS * SSM_STATE
SSM_TILE = 256
DIFF_HEADS = 8
DIFF_HEAD_DIM = 64
DIFF_V_DIM = 2 * DIFF_HEAD_DIM
DIFF_TILE = 256
DIFF_GROUP = 4
SGU_BLOCK = 128
SGU_GROUPS = 8
FF_CHUNK = 1024


def _params(*sem):
    return pltpu.CompilerParams(dimension_semantics=sem, vmem_limit_bytes=VMEM_LIMIT)


def _rms(x, g, n=None):
    ms = jnp.mean(x * x, axis=-1, keepdims=True)
    return x * lax.rsqrt(ms + EPS) * g


def _resident(shape):
    nd = len(shape)
    return pl.BlockSpec(shape, lambda *_: (0,) * nd, pipeline_mode=pl.Buffered(1))


def _norm_matmul_body(x_ref, g_ref, w_ref, o_ref, h_ref):
    @pl.when(pl.program_id(1) == 0)
    def _():
        h_ref[...] = _rms(x_ref[...], g_ref[...]).astype(BF16)

    o_ref[...] = jnp.dot(h_ref[...], w_ref[...], preferred_element_type=F32).astype(o_ref.dtype)


def norm_matmul(x, g, w, *, tm, tn, out_dtype):
    t, d = x.shape
    n = w.shape[1]
    assert t % tm == 0 and n % tn == 0
    return pl.pallas_call(
        _norm_matmul_body,
        grid=(t // tm, n // tn),
        in_specs=[pl.BlockSpec((tm, d), lambda i, j: (i, 0)),
                  pl.BlockSpec((1, d), lambda i, j: (0, 0)),
                  pl.BlockSpec((d, tn), lambda i, j: (0, j))],
        out_specs=pl.BlockSpec((tm, tn), lambda i, j: (i, j)),
        out_shape=jax.ShapeDtypeStruct((t, n), out_dtype),
        scratch_shapes=[pltpu.VMEM((tm, d), BF16)],
        compiler_params=_params("parallel", "arbitrary"),
        name="norm_matmul",
    )(x, g.reshape(1, d), w)


def _cross_body(q_ref, kv_ref, gq_ref, gk_ref, o_ref, kn_ref, vb_ref):
    hd = X_HEAD_DIM

    @pl.when(pl.program_id(1) == 0)
    def _():
        for h in range(X_HEADS):
            k = kv_ref[:, h * hd:(h + 1) * hd].astype(F32)
            kn_ref[:, h * hd:(h + 1) * hd] = _rms(k, gk_ref[...]).astype(BF16)
        vb_ref[...] = kv_ref[:, X_WIDTH:].astype(BF16)

    for h in range(X_HEADS):
        sl = slice(h * hd, (h + 1) * hd)
        q = _rms(q_ref[:, sl].astype(F32), gq_ref[...]) * (hd ** -0.5)
        s = lax.dot_general(q.astype(BF16), kn_ref[:, sl], (((1,), (1,)), ((), ())),
                            preferred_element_type=F32)
        p = jnp.exp(s - jnp.max(s, axis=-1, keepdims=True))
        l = jnp.sum(p, axis=-1, keepdims=True)
        o = jnp.dot(p.astype(BF16), vb_ref[:, sl], preferred_element_type=F32)
        o_ref[:, sl] = (o / l).astype(o_ref.dtype)


def cross_attention(proj, q_col, kv_all, layer, gq, gk, *, batch, tq, out_dtype):
    t = proj.shape[0]
    seq = t // batch
    m = kv_all.shape[0] // batch
    nq = seq // tq
    return pl.pallas_call(
        _cross_body,
        grid=(batch, nq),
        in_specs=[pl.BlockSpec((tq, X_WIDTH), lambda b, i: (b * nq + i, q_col)),
                  pl.BlockSpec((m, 2 * X_WIDTH), lambda b, i: (b, layer)),
                  pl.BlockSpec((1, X_HEAD_DIM), lambda b, i: (0, 0)),
                  pl.BlockSpec((1, X_HEAD_DIM), lambda b, i: (0, 0))],
        out_specs=pl.BlockSpec((tq, X_WIDTH), lambda b, i: (b * nq + i, 0)),
        out_shape=jax.ShapeDtypeStruct((t, X_WIDTH), out_dtype),
        scratch_shapes=[pltpu.VMEM((m, X_WIDTH), BF16), pltpu.VMEM((m, X_WIDTH), BF16)],
        compiler_params=_params("parallel", "arbitrary"),
        name="cross_attention",
    )(proj, kv_all, gq.reshape(1, -1), gk.reshape(1, -1))


def _out_proj_body(x_ref, mix_ref, cross_ref, wa_ref, wb_ref, o_ref):
    acc = x_ref[...]
    acc = acc + jnp.dot(mix_ref[...].astype(BF16), wa_ref[...], preferred_element_type=F32)
    acc = acc + jnp.dot(cross_ref[...].astype(BF16), wb_ref[...], preferred_element_type=F32)
    o_ref[...] = acc


def out_proj(x, mix, cross, wa, wb, *, tm):
    t, d = x.shape
    wm, wc = mix.shape[1], cross.shape[1]
    return pl.pallas_call(
        _out_proj_body,
        grid=(t // tm,),
        in_specs=[pl.BlockSpec((tm, d), lambda i: (i, 0)),
                  pl.BlockSpec((tm, wm), lambda i: (i, 0)),
                  pl.BlockSpec((tm, wc), lambda i: (i, 0)),
                  _resident((wm, d)),
                  _resident((wc, d))],
        out_specs=pl.BlockSpec((tm, d), lambda i: (i, 0)),
        out_shape=jax.ShapeDtypeStruct((t, d), F32),
        compiler_params=_params("parallel"),
        name="out_proj",
    )(x, mix, cross, wa, wb)


def _ffn_body(x_ref, g_ref, w1_ref, w2_ref, o_ref):
    x = x_ref[...]
    h = _rms(x, g_ref[...]).astype(BF16)
    acc = x
    for c in range(w1_ref.shape[1] // FF_CHUNK):
        sl = slice(c * FF_CHUNK, (c + 1) * FF_CHUNK)
        a = jnp.maximum(jnp.dot(h, w1_ref[:, sl], preferred_element_type=F32), 0.0)
        acc = acc + jnp.dot((a * a).astype(BF16), w2_ref[sl, :], preferred_element_type=F32)
    o_ref[...] = acc


def ffn(x, g, w1, w2, *, tm):
    t, d = x.shape
    f = w1.shape[1]
    assert f % FF_CHUNK == 0
    return pl.pallas_call(
        _ffn_body,
        grid=(t // tm,),
        in_specs=[pl.BlockSpec((tm, d), lambda i: (i, 0)),
                  _resident((1, d)),
                  _resident((d, f)),
                  _resident((f, d))],
        out_specs=pl.BlockSpec((tm, d), lambda i: (i, 0)),
        out_shape=jax.ShapeDtypeStruct((t, d), F32),
        compiler_params=_params("parallel"),
        name="ffn",
    )(x, g.reshape(1, d), w1, w2)


def _gelu(x):
    return 0.5 * x * (1.0 + lax.erf(x * (2.0 ** -0.5)))


def _sgu_body(u_ref, v_ref, vn_ref, ws_ref, bs_ref, o_ref):
    tm = u_ref.shape[0]
    blk = SGU_BLOCK
    gd = v_ref.shape[1] // SGU_GROUPS
    row = lax.broadcasted_iota(jnp.int32, (blk, blk), 0)
    col = lax.broadcasted_iota(jnp.int32, (blk, blk), 1)
    allowed = (col // CHUNK) <= (row // CHUNK)
    for g in range(SGU_GROUPS):
        sl = slice(g * gd, (g + 1) * gd)
        vg = _rms(_gelu(v_ref[:, sl].astype(F32)), vn_ref[:, sl]).astype(BF16)
        ws = jnp.where(allowed, ws_ref[g], 0.0).astype(BF16)
        bias = bs_ref[:, g:g + 1]
        for n in range(tm // blk):
            rs = slice(n * blk, (n + 1) * blk)
            mixed = jnp.dot(ws, vg[rs, :], preferred_element_type=F32) + bias
            o_ref[rs, sl] = (_gelu(u_ref[rs, sl].astype(F32)) * mixed).astype(o_ref.dtype)


def sgu_mixer(proj, v_norm, w_s, b_s, *, tm, out_dtype):
    t = proj.shape[0]
    w = v_norm.shape[0]
    return pl.pallas_call(
        _sgu_body,
        grid=(t // tm,),
        in_specs=[pl.BlockSpec((tm, w), lambda i: (i, 0)),
                  pl.BlockSpec((tm, w), lambda i: (i, 1)),
                  _resident((1, w)),
                  _resident((SGU_GROUPS, SGU_BLOCK, SGU_BLOCK)),
                  _resident((SGU_BLOCK, SGU_GROUPS))],
        out_specs=pl.BlockSpec((tm, w), lambda i: (i, 0)),
        out_shape=jax.ShapeDtypeStruct((t, w), out_dtype),
        compiler_params=_params("parallel"),
        name="sgu_mixer",
    )(proj, proj, v_norm.reshape(1, w), w_s, b_s.T)


def _half_rms(x, g, first_half):
    sq = x * x
    s1 = jnp.sum(jnp.where(first_half, sq, 0.0), axis=-1, keepdims=True)
    s2 = jnp.sum(jnp.where(first_half, 0.0, sq), axis=-1, keepdims=True)
    inv = 1.0 / DIFF_HEAD_DIM
    r = jnp.where(first_half, lax.rsqrt(s1 * inv + EPS), lax.rsqrt(s2 * inv + EPS))
    return x * r * g


N_SLOPE_PARTS = 3


def _diff_attn_body(slopes_ref, q_ref, k_ref, v_ref, qg_ref, kg_ref, lq1_ref, lk1_ref, lq2_ref,
                    lk2_ref, sub_ref, o_ref, ka_ref, vt_ref, qa_ref, m_ref, l_ref, acc_ref, *,
                    lambda_init, tq):
    tk = tq
    group = q_ref.shape[0] // tq
    seq = k_ref.shape[0]
    hd = DIFF_HEAD_DIM
    h = pl.program_id(1)
    gi = pl.program_id(2)
    slope = slopes_ref[h, N_SLOPE_PARTS]
    lane = lax.broadcasted_iota(jnp.int32, (1, LANES), 1)
    first_half = lane < hd
    row = lax.broadcasted_iota(jnp.int32, (LANES, 1), 0)

    @pl.when(gi == 0)
    def _():
        kn = _half_rms(k_ref[...].astype(F32), kg_ref[...], first_half)
        pos = (lax.broadcasted_iota(jnp.int32, (seq, 1), 0) & (tk - 1)).astype(F32)
        ka_ref[0] = jnp.where(first_half, kn,
                              jnp.where(lane < hd + N_SLOPE_PARTS, pos, 0.0)).astype(BF16)
        ka_ref[1] = jnp.where(first_half, jnp.where(lane < N_SLOPE_PARTS, pos, 0.0), kn).astype(BF16)
        for jb in range(seq // tk):
            vt_ref[jb] = v_ref[jb * tk:(jb + 1) * tk, :].astype(F32).T.astype(BF16)

    def slope_rows(base):
        out = jnp.zeros((LANES, 1), F32)
        for part in range(N_SLOPE_PARTS):
            out = jnp.where(row == base + part, slopes_ref[h, part], out)
        return out

    qn = _half_rms(q_ref[...].astype(F32), qg_ref[...], first_half) * (hd ** -0.5)
    qt = qn.T
    qa_ref[0] = jnp.where(row < hd, qt, slope_rows(hd)).astype(BF16)
    qa_ref[1] = jnp.where(row < hd, slope_rows(0), qt).astype(BF16)

    m_ref[...] = jnp.full(m_ref.shape, -jnp.inf, F32)
    l_ref[...] = jnp.zeros(l_ref.shape, F32)
    acc_ref[...] = jnp.zeros(acc_ref.shape, F32)

    tile_of_lane = lax.broadcasted_iota(jnp.int32, (1, group * tq), 1) // tq
    lane_offset = -slope * (tile_of_lane * tk).astype(F32)

    def update(j, first, extra, tile_bias):
        cols = slice(first * tq, group * tq)
        start = pl.multiple_of(j * tk, tk)
        vt = vt_ref[j]
        s = [jnp.dot(ka_ref[mp, pl.ds(start, tk), :], qa_ref[mp, :, cols],
                     preferred_element_type=F32) for mp in range(2)]
        if extra is not None:
            s = [x + extra for x in s]
        m_old = [m_ref[mp, :, cols] for mp in range(2)]
        m_new = [jnp.maximum(m_old[mp], jnp.max(s[mp], axis=0, keepdims=True) + tile_bias)
                 for mp in range(2)]
        p = [jnp.exp(s[mp] - (m_new[mp] - tile_bias)) for mp in range(2)]
        pv = [jnp.dot(vt, p[mp].astype(BF16), preferred_element_type=F32) for mp in range(2)]
        for mp in range(2):
            alpha = jnp.exp(m_old[mp] - m_new[mp])
            l_ref[mp, :, cols] = alpha * l_ref[mp, :, cols] + jnp.sum(p[mp], axis=0, keepdims=True)
            acc_ref[mp, :, cols] = alpha * acc_ref[mp, :, cols] + pv[mp]
            m_ref[mp, :, cols] = m_new[mp]

    def past_tile(j, carry):
        base = slope * jnp.full((1, 1), (j - gi * group) * tk, jnp.int32).astype(F32)
        update(j, 0, None, base + lane_offset)
        return carry

    lax.fori_loop(0, gi * group, past_tile, 0)

    c = lax.broadcasted_iota(jnp.int32, (tk, tq), 0)
    r = lax.broadcasted_iota(jnp.int32, (tk, tq), 1)
    ahead = jnp.where((c // CHUNK) == (r // CHUNK), (2.0 * slope) * (r - c).astype(F32), -jnp.inf)
    diag_extra = jnp.concatenate(
        [jnp.where(c <= r, 0.0, ahead), jnp.zeros((tk, (group - 1) * tq), F32)], axis=1)
    for b in range(group):
        width = (group - b) * tq
        update(gi * group + b, b, diag_extra[:, :width], lane_offset[:, :width])

    lam = (jnp.exp(jnp.sum(lq1_ref[...] * lk1_ref[...], axis=-1, keepdims=True))
           - jnp.exp(jnp.sum(lq2_ref[...] * lk2_ref[...], axis=-1, keepdims=True)) + lambda_init)
    ot = acc_ref[0] / l_ref[0] - lam * (acc_ref[1] / l_ref[1])
    o_ref[...] = (_rms(ot.T, sub_ref[...]) * (1.0 - lambda_init)).astype(o_ref.dtype)


def diff_attention(proj, q_norm, k_norm, lq1, lk1, lq2, lk2, sub_norm, lambda_init, *, batch,
                   out_dtype):
    t = proj.shape[0]
    seq = t // batch
    tq = min(DIFF_TILE, seq)
    tg = min(DIFF_GROUP * tq, seq)
    ng = seq // tg
    hw = DIFF_V_DIM
    assert tq & (tq - 1) == 0 and tq <= 256
    slope = jnp.exp2(-8.0 * jnp.arange(1, DIFF_HEADS + 1, dtype=F32) / DIFF_HEADS)
    parts, rest = [], slope
    for _ in range(N_SLOPE_PARTS):
        parts.append(rest.astype(BF16).astype(F32))
        rest = rest - parts[-1]
    slopes = jnp.stack(parts + [slope], axis=1)
    vec = lambda a: a.reshape(1, -1).astype(F32)
    two = lambda a: jnp.tile(a.reshape(1, -1).astype(F32), (1, 2))
    small = lambda n: pl.BlockSpec((1, n), lambda b, h, i: (0, 0))
    return pl.pallas_call(
        functools.partial(_diff_attn_body, lambda_init=lambda_init, tq=tq),
        grid=(batch, DIFF_HEADS, ng),
        in_specs=[pl.BlockSpec(memory_space=pltpu.SMEM),
                  pl.BlockSpec((tg, hw), lambda b, h, i: (b * ng + i, h)),
                  pl.BlockSpec((seq, hw), lambda b, h, i: (b, DIFF_HEADS + h)),
                  pl.BlockSpec((seq, hw), lambda b, h, i: (b, 2 * DIFF_HEADS + h)),
                  small(hw), small(hw),
                  small(DIFF_HEAD_DIM), small(DIFF_HEAD_DIM), small(DIFF_HEAD_DIM), small(DIFF_HEAD_DIM),
                  small(hw)],
        out_specs=pl.BlockSpec((tg, hw), lambda b, h, i: (b * ng + i, h)),
        out_shape=jax.ShapeDtypeStruct((t, DIFF_HEADS * hw), out_dtype),
        scratch_shapes=[pltpu.VMEM((2, seq, hw), BF16), pltpu.VMEM((seq // tq, hw, tq), BF16),
                        pltpu.VMEM((2, hw, tg), BF16),
                        pltpu.VMEM((2, 1, tg), F32), pltpu.VMEM((2, 1, tg), F32),
                        pltpu.VMEM((2, hw, tg), F32)],
        compiler_params=_params("parallel", "parallel", "arbitrary"),
        name="diff_attention",
    )(slopes, proj, proj, proj, two(q_norm), two(k_norm), vec(lq1), vec(lk1), vec(lq2), vec(lk2),
      vec(sub_norm))


def _silu(x):
    return x * (1.0 / (1.0 + jnp.exp(-x)))


def _softplus(x):
    return jnp.maximum(x, 0.0) + jnp.log1p(jnp.exp(-jnp.abs(x)))


def _pair_expand(a, h, first_half):
    return jnp.where(first_half, a[:, h:h + 1], a[:, h + 1:h + 2])


def _ssm_body(z_ref, xs_ref, bc_ref, dt_ref, cwx_ref, cwb_ref, cbx_ref, cbb_ref, dtb_ref, alog_ref,
              dskip_ref, gn_ref, o_ref, xcat_ref, bcat_ref, state_ref, xc_ref, bcc_ref, e_ref):
    q = z_ref.shape[0]
    pad = 8
    first_half = lax.broadcasted_iota(jnp.int32, (1, LANES), 1) < SSM_HEAD_DIM

    @pl.when(pl.program_id(1) == 0)
    def _():
        xcat_ref[0:pad, :] = jnp.zeros((pad, xcat_ref.shape[1]), F32)
        bcat_ref[0:pad, :] = jnp.zeros((pad, bcat_ref.shape[1]), F32)
        state_ref[...] = jnp.zeros(state_ref.shape, F32)

    def conv_silu(src_ref, cat_ref, w_ref, b_ref, dst_ref):
        cat_ref[pad:pad + q, :] = src_ref[...].astype(F32)
        acc = b_ref[...] + w_ref[0:1, :] * cat_ref[pad - 3:pad - 3 + q, :]
        for k in range(1, SSM_CONV):
            acc = acc + w_ref[k:k + 1, :] * cat_ref[pad - 3 + k:pad - 3 + k + q, :]
        dst_ref[...] = _silu(acc)
        cat_ref[0:pad, :] = cat_ref[q:q + pad, :]

    conv_silu(xs_ref, xcat_ref, cwx_ref, cbx_ref, xc_ref)
    conv_silu(bc_ref, bcat_ref, cwb_ref, cbb_ref, bcc_ref)

    dtv = _softplus(dt_ref[...] + dtb_ref[...])
    adt = dtv * (-jnp.exp(alog_ref[...]))
    row = lax.broadcasted_iota(jnp.int32, (q, q), 0)
    col = lax.broadcasted_iota(jnp.int32, (q, q), 1)
    causal = col <= row
    cs = jnp.dot(causal.astype(F32), adt, preferred_element_type=F32, precision=lax.Precision.HIGHEST)
    cs_t = cs.T
    cs_last = cs[q - 1:q, :]
    dfs = jnp.exp(cs)
    dte = jnp.exp(cs_last - cs)
    cd = jnp.exp(cs_last)

    for pr in range(SSM_HEADS // 2):
        h = 2 * pr
        e_ref[0, :, pr * LANES:(pr + 1) * LANES] = _pair_expand(dtv, h, first_half)
        e_ref[1, :, pr * LANES:(pr + 1) * LANES] = _pair_expand(dfs, h, first_half)
        e_ref[2, :, pr * LANES:(pr + 1) * LANES] = _pair_expand(dte, h, first_half)
        e_ref[3, 0:1, pr * LANES:(pr + 1) * LANES] = _pair_expand(cd, h, first_half)

    gw = SSM_HPG * SSM_HEAD_DIM
    for g in range(SSM_GROUPS):
        gs = slice(g * gw, (g + 1) * gw)
        bg = bcc_ref[:, g * SSM_STATE:(g + 1) * SSM_STATE].astype(BF16)
        cg = bcc_ref[:, SSM_BC + g * SSM_STATE:SSM_BC + (g + 1) * SSM_STATE].astype(BF16)
        cb = lax.dot_general(cg, bg, (((1,), (1,)), ((), ())), preferred_element_type=F32)
        xg = xc_ref[:, gs]
        xdt = xg * e_ref[0, :, gs]
        y = jnp.dot(cg, state_ref[g].astype(BF16), preferred_element_type=F32) * e_ref[1, :, gs]
        y = y + dskip_ref[:, gs] * xg
        yd = []
        for p2 in range(SSM_HPG // 2):
            xp = xdt[:, p2 * LANES:(p2 + 1) * LANES]
            acc = None
            for half in range(2):
                h = g * SSM_HPG + 2 * p2 + half
                decay = jnp.exp(jnp.where(causal, cs[:, h:h + 1] - cs_t[h:h + 1, :], -jnp.inf))
                m = (cb * decay).astype(BF16)
                keep = first_half if half == 0 else jnp.logical_not(first_half)
                part = jnp.dot(m, jnp.where(keep, xp, 0.0).astype(BF16), preferred_element_type=F32)
                acc = part if acc is None else acc + part
            yd.append(acc)
        y = y + jnp.concatenate(yd, axis=1)
        wgt = (xdt * e_ref[2, :, gs]).astype(BF16)
        state_ref[g] = state_ref[g] * e_ref[3, 0:1, gs] + lax.dot_general(
            bg, wgt, (((0,), (0,)), ((), ())), preferred_element_type=F32)
        yg = y * _silu(z_ref[:, gs].astype(F32))
        o_ref[:, gs] = _rms(yg, gn_ref[:, gs]).astype(o_ref.dtype)


def ssm_mixer(proj, dt_raw, conv_w, conv_b, dt_bias, a_log, d_skip, gate_norm, *, batch, out_dtype):
    t = proj.shape[0]
    seq = t // batch
    q = min(SSM_TILE, seq)
    nt = seq // q
    di = SSM_D_INNER
    pad_lanes = lambda a: jnp.pad(a.reshape(1, -1).astype(F32), ((0, 0), (0, LANES - a.shape[-1])))
    tile = lambda c: pl.BlockSpec((q, di), lambda b, i: (b * nt + i, c))
    return pl.pallas_call(
        _ssm_body,
        grid=(batch, nt),
        in_specs=[tile(0), tile(1), tile(2),
                  pl.BlockSpec((q, LANES), lambda b, i: (b * nt + i, 0)),
                  _resident((SSM_CONV, di)), _resident((SSM_CONV, di)),
                  _resident((1, di)), _resident((1, di)),
                  _resident((1, LANES)), _resident((1, LANES)),
                  _resident((1, di)), _resident((1, di))],
        out_specs=pl.BlockSpec((q, di), lambda b, i: (b * nt + i, 0)),
        out_shape=jax.ShapeDtypeStruct((t, di), out_dtype),
        scratch_shapes=[pltpu.VMEM((q + 8, di), F32), pltpu.VMEM((q + 8, di), F32),
                        pltpu.VMEM((SSM_GROUPS, SSM_STATE, SSM_HPG * SSM_HEAD_DIM), F32),
                        pltpu.VMEM((q, di), F32), pltpu.VMEM((q, di), F32),
                        pltpu.VMEM((4, q, di), F32)],
        compiler_params=_params("parallel", "arbitrary"),
        name="ssm_mixer",
    )(proj, proj, proj, dt_raw,
      conv_w[:, :di], conv_w[:, di:], conv_b[:di].reshape(1, di), conv_b[di:].reshape(1, di),
      pad_lanes(dt_bias), pad_lanes(a_log),
      jnp.repeat(d_skip.astype(F32), SSM_HEAD_DIM).reshape(1, di), gate_norm.reshape(1, di))


def kernel(x, mem, norm_mix, norm_ffn, mem_norm, w_mem_kv, xq_norm, xk_norm, w_ff1, w_ff2, w_in_ssm, ssm_conv_w, ssm_conv_b, ssm_dt_bias, ssm_a_log, ssm_d, ssm_gate_norm, w_out_ssm, w_in_diff, diff_q_norm, diff_k_norm, diff_lq1, diff_lk1, diff_lq2, diff_lk2, diff_sub_norm, w_out_diff, w_in_sgu, sgu_v_norm, sgu_w_s, sgu_b_s, w_out_sgu):
    batch, seq, d = x.shape
    depth = norm_mix.shape[0]
    t = batch * seq
    act = F32
    tm = min(512, t)
    xf = x.reshape(t, d)

    mem_f = mem.reshape(-1, d)
    w_kv_all = jnp.moveaxis(w_mem_kv, 0, 1).reshape(d, -1).astype(BF16)
    kv_all = norm_matmul(mem_f, mem_norm, w_kv_all, tm=min(512, mem_f.shape[0]), tn=2 * X_WIDTH,
                         out_dtype=act)

    for i in range(depth):
        kind, j = i % N_MIXERS, i // N_MIXERS
        if kind == 0:
            w_in = w_in_ssm[j]
            n_main = 2 * SSM_D_INNER + 2 * SSM_BC
            w_main = jnp.concatenate([w_in[:, :n_main], w_in[:, -X_WIDTH:]], axis=1).astype(BF16)
            w_dt = jnp.pad(w_in[:, n_main:n_main + SSM_HEADS], ((0, 0), (0, LANES - SSM_HEADS))).astype(BF16)
            proj = norm_matmul(xf, norm_mix[i], w_main, tm=tm, tn=1792, out_dtype=act)
            dt_raw = norm_matmul(xf, norm_mix[i], w_dt, tm=tm, tn=LANES, out_dtype=F32)
            mix = ssm_mixer(proj, dt_raw, ssm_conv_w[j], ssm_conv_b[j], ssm_dt_bias[j], ssm_a_log[j],
                            ssm_d[j], ssm_gate_norm[j], batch=batch, out_dtype=act)
            q_col = n_main // X_WIDTH
            w_out = w_out_ssm[j]
        elif kind == 1:
            proj = norm_matmul(xf, norm_mix[i], w_in_diff[j].astype(BF16), tm=tm, tn=1024, out_dtype=act)
            lambda_init = 0.8 - 0.6 * math.exp(-0.3 * i)
            mix = diff_attention(proj, diff_q_norm[j], diff_k_norm[j], diff_lq1[j], diff_lk1[j],
                                 diff_lq2[j], diff_lk2[j], diff_sub_norm[j], lambda_init, batch=batch,
                                 out_dtype=act)
            q_col = 3
            w_out = w_out_diff[j]
        else:
            proj = norm_matmul(xf, norm_mix[i], w_in_sgu[j].astype(BF16), tm=tm, tn=1024, out_dtype=act)
            mix = sgu_mixer(proj, sgu_v_norm[j], sgu_w_s[j], sgu_b_s[j], tm=tm, out_dtype=act)
            q_col = 2
            w_out = w_out_sgu[j]
        cross = cross_attention(proj, q_col, kv_all, i, xq_norm[i], xk_norm[i], batch=batch,
                                tq=min(512, seq), out_dtype=act)
        wm = mix.shape[1]
        xf = out_proj(xf, mix, cross, w_out[:wm].astype(BF16), w_out[wm:].astype(BF16), tm=tm)
        xf = ffn(xf, norm_ffn[i], w_ff1[i].astype(BF16), w_ff2[i].astype(BF16), tm=tm)
    return xf.reshape(batch, seq, d)
```

```python
import functools
import math

import jax
import jax.numpy as jnp
from jax import lax
from jax.experimental import pallas as pl
from jax.experimental.pallas import tpu as pltpu

F32 = jnp.float32
BF16 = jnp.bfloat16

EPS = 1e-6
CHUNK = 64
N_MIXERS = 3
LANES = 128
VMEM_LIMIT = 56 * 1024 * 1024

X_HEADS = 4
X_HEAD_DIM = 256
X_WIDTH = X_HEADS * X_HEAD_DIM
SSM_HEAD_DIM = 64
SSM_HEADS = 32
SSM_GROUPS = 8
SSM_HPG = SSM_HEADS // SSM_GROUPS
SSM_STATE = 128
SSM_CONV = 4
SSM_D_INNER = SSM_HEADS * SSM_HEAD_DIM
SSM_BC = SSM_GROUPS * SSM_STATE
SSM_TILE = 256
DIFF_HEADS = 8
DIFF_HEAD_DIM = 64
DIFF_V_DIM = 2 * DIFF_HEAD_DIM
DIFF_TILE = 256
DIFF_GROUP = 4
SGU_BLOCK = 128
SGU_GROUPS = 8
FF_CHUNK = 1024


def _params(*sem):
    return pltpu.CompilerParams(dimension_semantics=sem, vmem_limit_bytes=VMEM_LIMIT)


def _rms(x, g, n=None):
    ms = jnp.mean(x * x, axis=-1, keepdims=True)
    return x * lax.rsqrt(ms + EPS) * g


def _resident(shape):
    nd = len(shape)
    return pl.BlockSpec(shape, lambda *_: (0,) * nd, pipeline_mode=pl.Buffered(1))


def _norm_matmul_body(x_ref, g_ref, w_ref, *rest, n_chunk):
    o_ref = rest[-1] if len(rest) == 1 else rest[1]
    h = _rms(x_ref[...], g_ref[...]).astype(BF16)
    for c in range(w_ref.shape[1] // n_chunk):
        sl = slice(c * n_chunk, (c + 1) * n_chunk)
        o_ref[:, sl] = jnp.dot(h, w_ref[:, sl], preferred_element_type=F32).astype(o_ref.dtype)
    if len(rest) == 3:
        rest[2][...] = jnp.dot(h, rest[0][...], preferred_element_type=F32)


def norm_matmul(x, g, w, w_f32_out=None, *, tm, n_chunk, out_dtype):
    t, d = x.shape
    n = w.shape[1]
    assert t % tm == 0 and n % n_chunk == 0
    in_specs = [pl.BlockSpec((tm, d), lambda i: (i, 0)), _resident((1, d)), _resident((d, n))]
    out_specs = [pl.BlockSpec((tm, n), lambda i: (i, 0))]
    out_shape = [jax.ShapeDtypeStruct((t, n), out_dtype)]
    args = [x, g.reshape(1, d), w]
    if w_f32_out is not None:
        nx = w_f32_out.shape[1]
        in_specs.append(_resident((d, nx)))
        out_specs.append(pl.BlockSpec((tm, nx), lambda i: (i, 0)))
        out_shape.append(jax.ShapeDtypeStruct((t, nx), F32))
        args.append(w_f32_out)
    out = pl.pallas_call(
        functools.partial(_norm_matmul_body, n_chunk=n_chunk),
        grid=(t // tm,),
        in_specs=in_specs, out_specs=out_specs, out_shape=out_shape,
        compiler_params=_params("parallel"),
        name="norm_matmul",
    )(*args)
    return out if w_f32_out is not None else out[0]


def _cross_body(q_ref, kv_ref, gq_ref, gk_ref, o_ref, kn_ref, vb_ref):
    hd = X_HEAD_DIM

    @pl.when(pl.program_id(1) == 0)
    def _():
        for h in range(X_HEADS):
            k = kv_ref[:, h * hd:(h + 1) * hd].astype(F32)
            kn_ref[:, h * hd:(h + 1) * hd] = _rms(k, gk_ref[...]).astype(BF16)
        vb_ref[...] = kv_ref[:, X_WIDTH:].astype(BF16)

    for h in range(X_HEADS):
        sl = slice(h * hd, (h + 1) * hd)
        q = _rms(q_ref[:, sl].astype(F32), gq_ref[...]) * (hd ** -0.5)
        s = lax.dot_general(q.astype(BF16), kn_ref[:, sl], (((1,), (1,)), ((), ())),
                            preferred_element_type=F32)
        p = jnp.exp(s - jnp.max(s, axis=-1, keepdims=True))
        l = jnp.sum(p, axis=-1, keepdims=True)
        o = jnp.dot(p.astype(BF16), vb_ref[:, sl], preferred_element_type=F32)
        o_ref[:, sl] = (o / l).astype(o_ref.dtype)


def cross_attention(proj, q_col, kv_all, layer, gq, gk, *, batch, tq, out_dtype):
    t = proj.shape[0]
    seq = t // batch
    m = kv_all.shape[0] // batch
    nq = seq // tq
    return pl.pallas_call(
        _cross_body,
        grid=(batch, nq),
        in_specs=[pl.BlockSpec((tq, X_WIDTH), lambda b, i: (b * nq + i, q_col)),
                  pl.BlockSpec((m, 2 * X_WIDTH), lambda b, i: (b, layer)),
                  pl.BlockSpec((1, X_HEAD_DIM), lambda b, i: (0, 0)),
                  pl.BlockSpec((1, X_HEAD_DIM), lambda b, i: (0, 0))],
        out_specs=pl.BlockSpec((tq, X_WIDTH), lambda b, i: (b * nq + i, 0)),
        out_shape=jax.ShapeDtypeStruct((t, X_WIDTH), out_dtype),
        scratch_shapes=[pltpu.VMEM((m, X_WIDTH), BF16), pltpu.VMEM((m, X_WIDTH), BF16)],
        compiler_params=_params("parallel", "arbitrary"),
        name="cross_attention",
    )(proj, kv_all, gq.reshape(1, -1), gk.reshape(1, -1))


def _out_ffn_body(x_ref, mix_ref, cross_ref, wa_ref, wb_ref, g_ref, w1_ref, w2_ref, o_ref):
    x1 = x_ref[...]
    x1 = x1 + jnp.dot(mix_ref[...], wa_ref[...], preferred_element_type=F32)
    x1 = x1 + jnp.dot(cross_ref[...], wb_ref[...], preferred_element_type=F32)
    o_ref[...] = x1
    h = _rms(x1, g_ref[...]).astype(BF16)
    acc = None
    for c in range(w1_ref.shape[1] // FF_CHUNK):
        sl = slice(c * FF_CHUNK, (c + 1) * FF_CHUNK)
        a = jnp.maximum(jnp.dot(h, w1_ref[:, sl], preferred_element_type=F32), 0.0)
        y = jnp.dot((a * a).astype(BF16), w2_ref[sl, :], preferred_element_type=F32)
        acc = y if acc is None else acc + y
    o_ref[...] = o_ref[...] + acc


def out_proj_ffn(x, mix, cross, wa, wb, g, w1, w2, *, tm):
    t, d = x.shape
    wm, wc = mix.shape[1], cross.shape[1]
    f = w1.shape[1]
    assert f % FF_CHUNK == 0 and mix.dtype == BF16 and cross.dtype == BF16
    return pl.pallas_call(
        _out_ffn_body,
        grid=(t // tm,),
        in_specs=[pl.BlockSpec((tm, d), lambda i: (i, 0)),
                  pl.BlockSpec((tm, wm), lambda i: (i, 0)),
                  pl.BlockSpec((tm, wc), lambda i: (i, 0)),
                  _resident((wm, d)), _resident((wc, d)),
                  _resident((1, d)), _resident((d, f)), _resident((f, d))],
        out_specs=pl.BlockSpec((tm, d), lambda i: (i, 0)),
        out_shape=jax.ShapeDtypeStruct((t, d), F32),
        compiler_params=_params("parallel"),
        name="out_proj_ffn",
    )(x, mix, cross, wa, wb, g.reshape(1, d), w1, w2)


def _gelu(x):
    return 0.5 * x * (1.0 + lax.erf(x * (2.0 ** -0.5)))


def _sgu_body(u_ref, v_ref, vn_ref, ws_ref, bs_ref, o_ref):
    tm = u_ref.shape[0]
    blk = SGU_BLOCK
    gd = v_ref.shape[1] // SGU_GROUPS
    row = lax.broadcasted_iota(jnp.int32, (blk, blk), 0)
    col = lax.broadcasted_iota(jnp.int32, (blk, blk), 1)
    allowed = (col // CHUNK) <= (row // CHUNK)
    for g in range(SGU_GROUPS):
        sl = slice(g * gd, (g + 1) * gd)
        vg = _rms(_gelu(v_ref[:, sl].astype(F32)), vn_ref[:, sl]).astype(BF16)
        ws = jnp.where(allowed, ws_ref[g], 0.0).astype(BF16)
        bias = bs_ref[:, g:g + 1]
        for n in range(tm // blk):
            rs = slice(n * blk, (n + 1) * blk)
            mixed = jnp.dot(ws, vg[rs, :], preferred_element_type=F32) + bias
            o_ref[rs, sl] = (_gelu(u_ref[rs, sl].astype(F32)) * mixed).astype(o_ref.dtype)


def sgu_mixer(proj, v_norm, w_s, b_s, *, tm, out_dtype):
    t = proj.shape[0]
    w = v_norm.shape[0]
    return pl.pallas_call(
        _sgu_body,
        grid=(t // tm,),
        in_specs=[pl.BlockSpec((tm, w), lambda i: (i, 0)),
                  pl.BlockSpec((tm, w), lambda i: (i, 1)),
                  _resident((1, w)),
                  _resident((SGU_GROUPS, SGU_BLOCK, SGU_BLOCK)),
                  _resident((SGU_BLOCK, SGU_GROUPS))],
        out_specs=pl.BlockSpec((tm, w), lambda i: (i, 0)),
        out_shape=jax.ShapeDtypeStruct((t, w), out_dtype),
        compiler_params=_params("parallel"),
        name="sgu_mixer",
    )(proj, proj, v_norm.reshape(1, w), w_s, b_s.T)


def _half_rms(x, g, first_half):
    sq = x * x
    s1 = jnp.sum(jnp.where(first_half, sq, 0.0), axis=-1, keepdims=True)
    s2 = jnp.sum(jnp.where(first_half, 0.0, sq), axis=-1, keepdims=True)
    inv = 1.0 / DIFF_HEAD_DIM
    r = jnp.where(first_half, lax.rsqrt(s1 * inv + EPS), lax.rsqrt(s2 * inv + EPS))
    return x * r * g


LOG2E = 1.4426950408889634
N_SLOPE_PARTS = 3


def _diff_attn_body(slopes_ref, q_ref, k_ref, v_ref, qg_ref, kg_ref, lq1_ref, lk1_ref, lq2_ref,
                    lk2_ref, sub_ref, o_ref, ka_ref, vt_ref, qa_ref, sa_ref, sb_ref, m_ref, l_ref, acc_ref, *,
                    lambda_init, tq):
    tk = tq
    group = q_ref.shape[0] // tq
    seq = k_ref.shape[0]
    hd = DIFF_HEAD_DIM
    h = pl.program_id(1)
    gi = pl.program_id(2)
    slope = slopes_ref[h, N_SLOPE_PARTS]
    lane = lax.broadcasted_iota(jnp.int32, (1, LANES), 1)
    first_half = lane < hd
    row = lax.broadcasted_iota(jnp.int32, (LANES, 1), 0)

    @pl.when(gi == 0)
    def _():
        kn = _half_rms(k_ref[...].astype(F32), kg_ref[...], first_half)
        pos = (lax.broadcasted_iota(jnp.int32, (seq, 1), 0) & (tk - 1)).astype(F32)
        ka_ref[0] = jnp.where(first_half, kn,
                              jnp.where(lane < hd + N_SLOPE_PARTS, pos, 0.0)).astype(BF16)
        ka_ref[1] = jnp.where(first_half, jnp.where(lane < N_SLOPE_PARTS, pos, 0.0), kn).astype(BF16)
        for jb in range(seq // tk):
            vt_ref[jb] = v_ref[jb * tk:(jb + 1) * tk, :].astype(F32).T.astype(BF16)

    def slope_rows(base):
        out = jnp.zeros((LANES, 1), F32)
        for part in range(N_SLOPE_PARTS):
            out = jnp.where(row == base + part, slopes_ref[h, part], out)
        return out

    qn = _half_rms(q_ref[...].astype(F32), qg_ref[...], first_half) * (hd ** -0.5 * LOG2E)
    qt = qn.T
    qa_ref[0] = jnp.where(row < hd, qt, slope_rows(hd)).astype(BF16)
    qa_ref[1] = jnp.where(row < hd, slope_rows(0), qt).astype(BF16)

    m_ref[...] = jnp.full(m_ref.shape, -jnp.inf, F32)
    l_ref[...] = jnp.zeros(l_ref.shape, F32)
    acc_ref[...] = jnp.zeros(acc_ref.shape, F32)

    tile_of_lane = lax.broadcasted_iota(jnp.int32, (1, group * tq), 1) // tq
    lane_offset = -slope * (tile_of_lane * tk).astype(F32)

    def scores(j, first):
        start = pl.multiple_of(j * tk, tk)
        return [jnp.dot(ka_ref[mp, pl.ds(start, tk), :], qa_ref[mp, :, first * tq:],
                        preferred_element_type=F32) for mp in range(2)]

    def fold(j, first, s, extra, tile_bias):
        cols = slice(first * tq, group * tq)
        vt = vt_ref[j]
        if extra is not None:
            s = [jnp.concatenate([x[:, :tq] + extra, x[:, tq:]], axis=1) if x.shape[1] > tq
                 else x + extra for x in s]
        for mp in range(2):
            m_old = m_ref[mp, :, cols]
            m_new = jnp.maximum(m_old, jnp.max(s[mp], axis=0, keepdims=True) + tile_bias)
            p = jnp.exp2(s[mp] - (m_new - tile_bias))
            pv = jnp.dot(vt, p.astype(BF16), preferred_element_type=F32)
            alpha = jnp.exp2(m_old - m_new)
            l_ref[mp, :, cols] = alpha * l_ref[mp, :, cols] + jnp.sum(p, axis=0, keepdims=True)
            acc_ref[mp, :, cols] = alpha * acc_ref[mp, :, cols] + pv
            m_ref[mp, :, cols] = m_new

    def put(ref, s):
        ref[0], ref[1] = s[0], s[1]

    def past_bias(j):
        return slope * jnp.full((1, 1), (j - gi * group) * tk, jnp.int32).astype(F32) + lane_offset

    put(sa_ref, scores(0, 0))

    def past_pair(jj, carry):
        j = 2 * jj
        put(sb_ref, scores(j + 1, 0))
        fold(j, 0, [sa_ref[0], sa_ref[1]], None, past_bias(j))
        put(sa_ref, scores(j + 2, 0))
        fold(j + 1, 0, [sb_ref[0], sb_ref[1]], None, past_bias(j + 1))
        return carry

    lax.fori_loop(0, gi * group // 2, past_pair, 0)

    c = lax.broadcasted_iota(jnp.int32, (tk, tq), 0)
    r = lax.broadcasted_iota(jnp.int32, (tk, tq), 1)
    ahead = jnp.where((c // CHUNK) == (r // CHUNK), (2.0 * slope) * (r - c).astype(F32), -jnp.inf)
    diag_extra = jnp.where(c <= r, 0.0, ahead)
    s_cur = [sa_ref[0], sa_ref[1]]
    for b in range(group):
        s_next = scores(gi * group + b + 1, b + 1) if b + 1 < group else None
        width = (group - b) * tq
        fold(gi * group + b, b, s_cur, diag_extra, lane_offset[:, :width])
        s_cur = s_next

    lam = (jnp.exp(jnp.sum(lq1_ref[...] * lk1_ref[...], axis=-1, keepdims=True))
           - jnp.exp(jnp.sum(lq2_ref[...] * lk2_ref[...], axis=-1, keepdims=True)) + lambda_init)
    ot = acc_ref[0] / l_ref[0] - lam * (acc_ref[1] / l_ref[1])
    o_ref[...] = (_rms(ot.T, sub_ref[...]) * (1.0 - lambda_init)).astype(o_ref.dtype)


def diff_attention(proj, q_norm, k_norm, lq1, lk1, lq2, lk2, sub_norm, lambda_init, *, batch,
                   out_dtype):
    t = proj.shape[0]
    seq = t // batch
    tq = min(DIFF_TILE, seq)
    tg = min(DIFF_GROUP * tq, seq)
    ng = seq // tg
    hw = DIFF_V_DIM
    assert tq & (tq - 1) == 0 and tq <= 256
    assert ng == 1 or (tg // tq) % 2 == 0
    slope = LOG2E * jnp.exp2(-8.0 * jnp.arange(1, DIFF_HEADS + 1, dtype=F32) / DIFF_HEADS)
    parts, rest = [], slope
    for _ in range(N_SLOPE_PARTS):
        parts.append(rest.astype(BF16).astype(F32))
        rest = rest - parts[-1]
    slopes = jnp.stack(parts + [slope], axis=1)
    vec = lambda a: a.reshape(1, -1).astype(F32)
    two = lambda a: jnp.tile(a.reshape(1, -1).astype(F32), (1, 2))
    small = lambda n: pl.BlockSpec((1, n), lambda b, h, i: (0, 0))
    return pl.pallas_call(
        functools.partial(_diff_attn_body, lambda_init=lambda_init, tq=tq),
        grid=(batch, DIFF_HEADS, ng),
        in_specs=[pl.BlockSpec(memory_space=pltpu.SMEM),
                  pl.BlockSpec((tg, hw), lambda b, h, i: (b * ng + i, h)),
                  pl.BlockSpec((seq, hw), lambda b, h, i: (b, DIFF_HEADS + h)),
                  pl.BlockSpec((seq, hw), lambda b, h, i: (b, 2 * DIFF_HEADS + h)),
                  small(hw), small(hw),
                  small(DIFF_HEAD_DIM), small(DIFF_HEAD_DIM), small(DIFF_HEAD_DIM), small(DIFF_HEAD_DIM),
                  small(hw)],
        out_specs=pl.BlockSpec((tg, hw), lambda b, h, i: (b * ng + i, h)),
        out_shape=jax.ShapeDtypeStruct((t, DIFF_HEADS * hw), out_dtype),
        scratch_shapes=[pltpu.VMEM((2, seq, hw), BF16), pltpu.VMEM((seq // tq, hw, tq), BF16),
                        pltpu.VMEM((2, hw, tg), BF16),
                        pltpu.VMEM((2, tq, tg), F32), pltpu.VMEM((2, tq, tg), F32),
                        pltpu.VMEM((2, 1, tg), F32), pltpu.VMEM((2, 1, tg), F32),
                        pltpu.VMEM((2, hw, tg), F32)],
        compiler_params=_params("parallel", "parallel", "arbitrary"),
        name="diff_attention",
    )(slopes, proj, proj, proj, two(q_norm), two(k_norm), vec(lq1), vec(lk1), vec(lq2), vec(lk2),
      vec(sub_norm))


def _silu(x):
    h = 0.5 * x
    return h + h * jnp.tanh(h)


def _softplus(x):
    return jnp.maximum(x, 0.0) + jnp.log1p(jnp.exp(-jnp.abs(x)))


def _pair_expand(a, h, first_half):
    return jnp.where(first_half, a[:, h:h + 1], a[:, h + 1:h + 2])


def _ssm_body(z_ref, xs_ref, bc_ref, dt_ref, cwx_ref, cwb_ref, cbx_ref, cbb_ref, dtb_ref, alog_ref,
              dskip_ref, gn_ref, o_ref, xcat_ref, bcat_ref, state_ref, xc_ref, bcc_ref, e_ref):
    q = z_ref.shape[0]
    pad = 8
    first_half = lax.broadcasted_iota(jnp.int32, (1, LANES), 1) < SSM_HEAD_DIM

    @pl.when(pl.program_id(1) == 0)
    def _():
        xcat_ref[0:pad, :] = jnp.zeros((pad, xcat_ref.shape[1]), F32)
        bcat_ref[0:pad, :] = jnp.zeros((pad, bcat_ref.shape[1]), F32)
        state_ref[...] = jnp.zeros(state_ref.shape, F32)

    def conv_silu(src_ref, cat_ref, w_ref, b_ref, dst_ref):
        cat_ref[pad:pad + q, :] = src_ref[...].astype(F32)
        acc = b_ref[...] + w_ref[0:1, :] * cat_ref[pad - 3:pad - 3 + q, :]
        for k in range(1, SSM_CONV):
            acc = acc + w_ref[k:k + 1, :] * cat_ref[pad - 3 + k:pad - 3 + k + q, :]
        dst_ref[...] = _silu(acc)
        cat_ref[0:pad, :] = cat_ref[q:q + pad, :]

    conv_silu(xs_ref, xcat_ref, cwx_ref, cbx_ref, xc_ref)
    conv_silu(bc_ref, bcat_ref, cwb_ref, cbb_ref, bcc_ref)

    dtv = _softplus(dt_ref[...] + dtb_ref[...])
    adt = dtv * (-jnp.exp(alog_ref[...]))
    row = lax.broadcasted_iota(jnp.int32, (q, q), 0)
    col = lax.broadcasted_iota(jnp.int32, (q, q), 1)
    causal = col <= row
    cs = jnp.dot(causal.astype(F32), adt, preferred_element_type=F32, precision=lax.Precision.HIGHEST)
    cs_t = cs.T
    cs_last = cs[q - 1:q, :]
    dfs = jnp.exp(cs)
    dte = jnp.exp(cs_last - cs)
    cd = jnp.exp(cs_last)

    for pr in range(SSM_HEADS // 2):
        h = 2 * pr
        e_ref[0, :, pr * LANES:(pr + 1) * LANES] = _pair_expand(dtv, h, first_half)
        e_ref[1, :, pr * LANES:(pr + 1) * LANES] = _pair_expand(dfs, h, first_half)
        e_ref[2, :, pr * LANES:(pr + 1) * LANES] = _pair_expand(dte, h, first_half)
        e_ref[3, 0:1, pr * LANES:(pr + 1) * LANES] = _pair_expand(cd, h, first_half)

    gw = SSM_HPG * SSM_HEAD_DIM
    for g in range(SSM_GROUPS):
        gs = slice(g * gw, (g + 1) * gw)
        bg = bcc_ref[:, g * SSM_STATE:(g + 1) * SSM_STATE].astype(BF16)
        cg = bcc_ref[:, SSM_BC + g * SSM_STATE:SSM_BC + (g + 1) * SSM_STATE].astype(BF16)
        cb = lax.dot_general(cg, bg, (((1,), (1,)), ((), ())), preferred_element_type=F32)
        xg = xc_ref[:, gs]
        xdt = xg * e_ref[0, :, gs]
        y = jnp.dot(cg, state_ref[g].astype(BF16), preferred_element_type=F32) * e_ref[1, :, gs]
        y = y + dskip_ref[:, gs] * xg
        yd = []
        for p2 in range(SSM_HPG // 2):
            xp = xdt[:, p2 * LANES:(p2 + 1) * LANES]
            acc = None
            for half in range(2):
                h = g * SSM_HPG + 2 * p2 + half
                decay = jnp.exp(jnp.where(causal, cs[:, h:h + 1] - cs_t[h:h + 1, :], -jnp.inf))
                m = (cb * decay).astype(BF16)
                keep = first_half if half == 0 else jnp.logical_not(first_half)
                part = jnp.dot(m, jnp.where(keep, xp, 0.0).astype(BF16), preferred_element_type=F32)
                acc = part if acc is None else acc + part
            yd.append(acc)
        y = y + jnp.concatenate(yd, axis=1)
        wgt = (xdt * e_ref[2, :, gs]).astype(BF16)
        state_ref[g] = state_ref[g] * e_ref[3, 0:1, gs] + lax.dot_general(
            bg, wgt, (((0,), (0,)), ((), ())), preferred_element_type=F32)
        yg = y * _silu(z_ref[:, gs].astype(F32))
        o_ref[:, gs] = _rms(yg, gn_ref[:, gs]).astype(o_ref.dtype)


def ssm_mixer(proj, dt_raw, conv_w, conv_b, dt_bias, a_log, d_skip, gate_norm, *, batch, out_dtype):
    t = proj.shape[0]
    seq = t // batch
    q = min(SSM_TILE, seq)
    nt = seq // q
    di = SSM_D_INNER
    pad_lanes = lambda a: jnp.pad(a.reshape(1, -1).astype(F32), ((0, 0), (0, LANES - a.shape[-1])))
    tile = lambda c: pl.BlockSpec((q, di), lambda b, i: (b * nt + i, c))
    return pl.pallas_call(
        _ssm_body,
        grid=(batch, nt),
        in_specs=[tile(0), tile(1), tile(2),
                  pl.BlockSpec((q, LANES), lambda b, i: (b * nt + i, 0)),
                  _resident((SSM_CONV, di)), _resident((SSM_CONV, di)),
                  _resident((1, di)), _resident((1, di)),
                  _resident((1, LANES)), _resident((1, LANES)),
                  _resident((1, di)), _resident((1, di))],
        out_specs=pl.BlockSpec((q, di), lambda b, i: (b * nt + i, 0)),
        out_shape=jax.ShapeDtypeStruct((t, di), out_dtype),
        scratch_shapes=[pltpu.VMEM((q + 8, di), F32), pltpu.VMEM((q + 8, di), F32),
                        pltpu.VMEM((SSM_GROUPS, SSM_STATE, SSM_HPG * SSM_HEAD_DIM), F32),
                        pltpu.VMEM((q, di), F32), pltpu.VMEM((q, di), F32),
                        pltpu.VMEM((4, q, di), F32)],
        compiler_params=_params("parallel", "arbitrary"),
        name="ssm_mixer",
    )(proj, proj, proj, dt_raw,
      conv_w[:, :di], conv_w[:, di:], conv_b[:di].reshape(1, di), conv_b[di:].reshape(1, di),
      pad_lanes(dt_bias), pad_lanes(a_log),
      jnp.repeat(d_skip.astype(F32), SSM_HEAD_DIM).reshape(1, di), gate_norm.reshape(1, di))


def kernel(x, mem, norm_mix, norm_ffn, mem_norm, w_mem_kv, xq_norm, xk_norm, w_ff1, w_ff2, w_in_ssm, ssm_conv_w, ssm_conv_b, ssm_dt_bias, ssm_a_log, ssm_d, ssm_gate_norm, w_out_ssm, w_in_diff, diff_q_norm, diff_k_norm, diff_lq1, diff_lk1, diff_lq2, diff_lk2, diff_sub_norm, w_out_diff, w_in_sgu, sgu_v_norm, sgu_w_s, sgu_b_s, w_out_sgu):
    batch, seq, d = x.shape
    depth = norm_mix.shape[0]
    t = batch * seq
    act = BF16
    tm = min(512, t)
    xf = x.reshape(t, d)

    mem_f = mem.reshape(-1, d)
    w_kv_all = jnp.moveaxis(w_mem_kv, 0, 1).reshape(d, -1).astype(BF16)
    kv_all = norm_matmul(mem_f, mem_norm, w_kv_all, tm=min(256, mem_f.shape[0]), n_chunk=2 * X_WIDTH,
                         out_dtype=act)

    for i in range(depth):
        kind, j = i % N_MIXERS, i // N_MIXERS
        if kind == 0:
            w_in = w_in_ssm[j]
            n_main = 2 * SSM_D_INNER + 2 * SSM_BC
            w_main = jnp.concatenate([w_in[:, :n_main], w_in[:, -X_WIDTH:]], axis=1).astype(BF16)
            w_dt = jnp.pad(w_in[:, n_main:n_main + SSM_HEADS], ((0, 0), (0, LANES - SSM_HEADS))).astype(BF16)
            proj, dt_raw = norm_matmul(xf, norm_mix[i], w_main, w_dt, tm=tm, n_chunk=1024, out_dtype=act)
            mix = ssm_mixer(proj, dt_raw, ssm_conv_w[j], ssm_conv_b[j], ssm_dt_bias[j], ssm_a_log[j],
                            ssm_d[j], ssm_gate_norm[j], batch=batch, out_dtype=act)
            q_col = n_main // X_WIDTH
            w_out = w_out_ssm[j]
        elif kind == 1:
            proj = norm_matmul(xf, norm_mix[i], w_in_diff[j].astype(BF16), tm=tm, n_chunk=1024, out_dtype=act)
            lambda_init = 0.8 - 0.6 * math.exp(-0.3 * i)
            mix = diff_attention(proj, diff_q_norm[j], diff_k_norm[j], diff_lq1[j], diff_lk1[j],
                                 diff_lq2[j], diff_lk2[j], diff_sub_norm[j], lambda_init, batch=batch,
                                 out_dtype=act)
            q_col = 3
            w_out = w_out_diff[j]
        else:
            proj = norm_matmul(xf, norm_mix[i], w_in_sgu[j].astype(BF16), tm=tm, n_chunk=1024, out_dtype=act)
            mix = sgu_mixer(proj, sgu_v_norm[j], sgu_w_s[j], sgu_b_s[j], tm=tm, out_dtype=act)
            q_col = 2
            w_out = w_out_sgu[j]
        cross = cross_attention(proj, q_col, kv_all, i, xq_norm[i], xk_norm[i], batch=batch,
                                tq=min(512, seq), out_dtype=act)
        wm = mix.shape[1]
        xf = out_proj_ffn(xf, mix, cross, w_out[:wm].astype(BF16), w_out[wm:].astype(BF16),
                          norm_ffn[i], w_ff1[i].astype(BF16), w_ff2[i].astype(BF16), tm=tm)
    return xf.reshape(batch, seq, d)
```

```python
import functools
import math

import jax
import jax.numpy as jnp
from jax import lax
from jax.experimental import pallas as pl
from jax.experimental.pallas import tpu as pltpu

F32 = jnp.float32
BF16 = jnp.bfloat16

EPS = 1e-6
CHUNK = 64
N_MIXERS = 3
LANES = 128
VMEM_LIMIT = 56 * 1024 * 1024

X_HEADS = 4
X_HEAD_DIM = 256
X_WIDTH = X_HEADS * X_HEAD_DIM
SSM_HEAD_DIM = 64
SSM_HEADS = 32
SSM_GROUPS = 8
SSM_HPG = SSM_HEADS // SSM_GROUPS
SSM_STATE = 128
SSM_CONV = 4
SSM_D_INNER = SSM_HEADS * SSM_HEAD_DIM
SSM_BC = SSM_GROUPS * SSM_STATE
SSM_TILE = 256
DIFF_HEADS = 8
DIFF_HEAD_DIM = 64
DIFF_V_DIM = 2 * DIFF_HEAD_DIM
DIFF_TILE = 256
DIFF_GROUP = 4
SGU_BLOCK = 128
SGU_GROUPS = 8
FF_CHUNK = 1024


def _params(*sem):
    return pltpu.CompilerParams(dimension_semantics=sem, vmem_limit_bytes=VMEM_LIMIT)


def _rms(x, g, n=None):
    ms = jnp.mean(x * x, axis=-1, keepdims=True)
    return x * lax.rsqrt(ms + EPS) * g


def _resident(shape):
    nd = len(shape)
    return pl.BlockSpec(shape, lambda *_: (0,) * nd, pipeline_mode=pl.Buffered(1))


def _norm_matmul_body(x_ref, g_ref, w_ref, *rest, n_chunk):
    o_ref = rest[-1] if len(rest) == 1 else rest[1]
    h = _rms(x_ref[...], g_ref[...]).astype(BF16)
    for c in range(w_ref.shape[1] // n_chunk):
        sl = slice(c * n_chunk, (c + 1) * n_chunk)
        o_ref[:, sl] = jnp.dot(h, w_ref[:, sl], preferred_element_type=F32).astype(o_ref.dtype)
    if len(rest) == 3:
        rest[2][...] = jnp.dot(h, rest[0][...], preferred_element_type=F32)


def norm_matmul(x, g, w, w_f32_out=None, *, tm, n_chunk, out_dtype):
    t, d = x.shape
    n = w.shape[1]
    assert t % tm == 0 and n % n_chunk == 0
    in_specs = [pl.BlockSpec((tm, d), lambda i: (i, 0)), _resident((1, d)), _resident((d, n))]
    out_specs = [pl.BlockSpec((tm, n), lambda i: (i, 0))]
    out_shape = [jax.ShapeDtypeStruct((t, n), out_dtype)]
    args = [x, g.reshape(1, d), w]
    if w_f32_out is not None:
        nx = w_f32_out.shape[1]
        in_specs.append(_resident((d, nx)))
        out_specs.append(pl.BlockSpec((tm, nx), lambda i: (i, 0)))
        out_shape.append(jax.ShapeDtypeStruct((t, nx), F32))
        args.append(w_f32_out)
    out = pl.pallas_call(
        functools.partial(_norm_matmul_body, n_chunk=n_chunk),
        grid=(t // tm,),
        in_specs=in_specs, out_specs=out_specs, out_shape=out_shape,
        compiler_params=_params("parallel"),
        name="norm_matmul",
    )(*args)
    return out if w_f32_out is not None else out[0]


def _cross_body(q_ref, kv_ref, gq_ref, gk_ref, o_ref, kn_ref, vb_ref):
    hd = X_HEAD_DIM

    @pl.when(pl.program_id(1) == 0)
    def _():
        for h in range(X_HEADS):
            k = kv_ref[:, h * hd:(h + 1) * hd].astype(F32)
            kn_ref[:, h * hd:(h + 1) * hd] = _rms(k, gk_ref[...]).astype(BF16)
        vb_ref[...] = kv_ref[:, X_WIDTH:].astype(BF16)

    for h in range(X_HEADS):
        sl = slice(h * hd, (h + 1) * hd)
        q = _rms(q_ref[:, sl].astype(F32), gq_ref[...]) * (hd ** -0.5)
        s = lax.dot_general(q.astype(BF16), kn_ref[:, sl], (((1,), (1,)), ((), ())),
                            preferred_element_type=F32)
        p = jnp.exp(s - jnp.max(s, axis=-1, keepdims=True))
        l = jnp.sum(p, axis=-1, keepdims=True)
        o = jnp.dot(p.astype(BF16), vb_ref[:, sl], preferred_element_type=F32)
        o_ref[:, sl] = (o / l).astype(o_ref.dtype)


def cross_attention(proj, q_col, kv_all, layer, gq, gk, *, batch, tq, out_dtype):
    t = proj.shape[0]
    seq = t // batch
    m = kv_all.shape[0] // batch
    nq = seq // tq
    return pl.pallas_call(
        _cross_body,
        grid=(batch, nq),
        in_specs=[pl.BlockSpec((tq, X_WIDTH), lambda b, i: (b * nq + i, q_col)),
                  pl.BlockSpec((m, 2 * X_WIDTH), lambda b, i: (b, layer)),
                  pl.BlockSpec((1, X_HEAD_DIM), lambda b, i: (0, 0)),
                  pl.BlockSpec((1, X_HEAD_DIM), lambda b, i: (0, 0))],
        out_specs=pl.BlockSpec((tq, X_WIDTH), lambda b, i: (b * nq + i, 0)),
        out_shape=jax.ShapeDtypeStruct((t, X_WIDTH), out_dtype),
        scratch_shapes=[pltpu.VMEM((m, X_WIDTH), BF16), pltpu.VMEM((m, X_WIDTH), BF16)],
        compiler_params=_params("parallel", "arbitrary"),
        name="cross_attention",
    )(proj, kv_all, gq.reshape(1, -1), gk.reshape(1, -1))


def _out_ffn_body(x_ref, mix_ref, cross_ref, wa_ref, wb_ref, g_ref, w1_ref, w2_ref, o_ref):
    x1 = x_ref[...]
    x1 = x1 + jnp.dot(mix_ref[...], wa_ref[...], preferred_element_type=F32)
    x1 = x1 + jnp.dot(cross_ref[...], wb_ref[...], preferred_element_type=F32)
    o_ref[...] = x1
    h = _rms(x1, g_ref[...]).astype(BF16)
    acc = None
    for c in range(w1_ref.shape[1] // FF_CHUNK):
        sl = slice(c * FF_CHUNK, (c + 1) * FF_CHUNK)
        a = jnp.maximum(jnp.dot(h, w1_ref[:, sl], preferred_element_type=F32), 0.0)
        y = jnp.dot((a * a).astype(BF16), w2_ref[sl, :], preferred_element_type=F32)
        acc = y if acc is None else acc + y
    o_ref[...] = o_ref[...] + acc


def out_proj_ffn(x, mix, cross, wa, wb, g, w1, w2, *, tm):
    t, d = x.shape
    wm, wc = mix.shape[1], cross.shape[1]
    f = w1.shape[1]
    assert f % FF_CHUNK == 0 and mix.dtype == BF16 and cross.dtype == BF16
    return pl.pallas_call(
        _out_ffn_body,
        grid=(t // tm,),
        in_specs=[pl.BlockSpec((tm, d), lambda i: (i, 0)),
                  pl.BlockSpec((tm, wm), lambda i: (i, 0)),
                  pl.BlockSpec((tm, wc), lambda i: (i, 0)),
                  _resident((wm, d)), _resident((wc, d)),
                  _resident((1, d)), _resident((d, f)), _resident((f, d))],
        out_specs=pl.BlockSpec((tm, d), lambda i: (i, 0)),
        out_shape=jax.ShapeDtypeStruct((t, d), F32),
        compiler_params=_params("parallel"),
        name="out_proj_ffn",
    )(x, mix, cross, wa, wb, g.reshape(1, d), w1, w2)


def _gelu(x):
    return 0.5 * x * (1.0 + lax.erf(x * (2.0 ** -0.5)))


def _sgu_body(u_ref, v_ref, vn_ref, ws_ref, bs_ref, o_ref):
    tm = u_ref.shape[0]
    blk = SGU_BLOCK
    gd = v_ref.shape[1] // SGU_GROUPS
    row = lax.broadcasted_iota(jnp.int32, (blk, blk), 0)
    col = lax.broadcasted_iota(jnp.int32, (blk, blk), 1)
    allowed = (col // CHUNK) <= (row // CHUNK)
    for g in range(SGU_GROUPS):
        sl = slice(g * gd, (g + 1) * gd)
        vg = _rms(_gelu(v_ref[:, sl].astype(F32)), vn_ref[:, sl]).astype(BF16)
        ws = jnp.where(allowed, ws_ref[g], 0.0).astype(BF16)
        bias = bs_ref[:, g:g + 1]
        for n in range(tm // blk):
            rs = slice(n * blk, (n + 1) * blk)
            mixed = jnp.dot(ws, vg[rs, :], preferred_element_type=F32) + bias
            o_ref[rs, sl] = (_gelu(u_ref[rs, sl].astype(F32)) * mixed).astype(o_ref.dtype)


def sgu_mixer(proj, v_norm, w_s, b_s, *, tm, out_dtype):
    t = proj.shape[0]
    w = v_norm.shape[0]
    return pl.pallas_call(
        _sgu_body,
        grid=(t // tm,),
        in_specs=[pl.BlockSpec((tm, w), lambda i: (i, 0)),
                  pl.BlockSpec((tm, w), lambda i: (i, 1)),
                  _resident((1, w)),
                  _resident((SGU_GROUPS, SGU_BLOCK, SGU_BLOCK)),
                  _resident((SGU_BLOCK, SGU_GROUPS))],
        out_specs=pl.BlockSpec((tm, w), lambda i: (i, 0)),
        out_shape=jax.ShapeDtypeStruct((t, w), out_dtype),
        compiler_params=_params("parallel"),
        name="sgu_mixer",
    )(proj, proj, v_norm.reshape(1, w), w_s, b_s.T)


def _half_rms(x, g, first_half):
    sq = x * x
    s1 = jnp.sum(jnp.where(first_half, sq, 0.0), axis=-1, keepdims=True)
    s2 = jnp.sum(jnp.where(first_half, 0.0, sq), axis=-1, keepdims=True)
    inv = 1.0 / DIFF_HEAD_DIM
    r = jnp.where(first_half, lax.rsqrt(s1 * inv + EPS), lax.rsqrt(s2 * inv + EPS))
    return x * r * g


LOG2E = 1.4426950408889634
N_SLOPE_PARTS = 3


def _diff_attn_body(slopes_ref, q_ref, k_ref, v_ref, qg_ref, kg_ref, lq1_ref, lk1_ref, lq2_ref,
                    lk2_ref, sub_ref, o_ref, ka_ref, vt_ref, qa_ref, sa_ref, sb_ref, m_ref, l_ref, acc_ref, *,
                    lambda_init, tq):
    tk = tq
    group = q_ref.shape[0] // tq
    seq = k_ref.shape[0]
    hd = DIFF_HEAD_DIM
    h = pl.program_id(1)
    gi = pl.program_id(2)
    slope = slopes_ref[h, N_SLOPE_PARTS]
    lane = lax.broadcasted_iota(jnp.int32, (1, LANES), 1)
    first_half = lane < hd
    row = lax.broadcasted_iota(jnp.int32, (LANES, 1), 0)

    @pl.when(gi == 0)
    def _():
        kn = _half_rms(k_ref[...].astype(F32), kg_ref[...], first_half)
        pos = (lax.broadcasted_iota(jnp.int32, (seq, 1), 0) & (tk - 1)).astype(F32)
        ka_ref[0] = jnp.where(first_half, kn,
                              jnp.where(lane < hd + N_SLOPE_PARTS, pos, 0.0)).astype(BF16)
        ka_ref[1] = jnp.where(first_half, jnp.where(lane < N_SLOPE_PARTS, pos, 0.0), kn).astype(BF16)
        for jb in range(seq // tk):
            vt_ref[jb] = v_ref[jb * tk:(jb + 1) * tk, :].astype(F32).T.astype(BF16)

    def slope_rows(base):
        out = jnp.zeros((LANES, 1), F32)
        for part in range(N_SLOPE_PARTS):
            out = jnp.where(row == base + part, slopes_ref[h, part], out)
        return out

    qn = _half_rms(q_ref[...].astype(F32), qg_ref[...], first_half) * (hd ** -0.5 * LOG2E)
    qt = qn.T
    qa_ref[0] = jnp.where(row < hd, qt, slope_rows(hd)).astype(BF16)
    qa_ref[1] = jnp.where(row < hd, slope_rows(0), qt).astype(BF16)

    m_ref[...] = jnp.full(m_ref.shape, -jnp.inf, F32)
    l_ref[...] = jnp.zeros(l_ref.shape, F32)
    acc_ref[...] = jnp.zeros(acc_ref.shape, F32)

    tile_of_lane = lax.broadcasted_iota(jnp.int32, (1, group * tq), 1) // tq
    lane_offset = -slope * (tile_of_lane * tk).astype(F32)

    def scores(j, first):
        start = pl.multiple_of(j * tk, tk)
        return [jnp.dot(ka_ref[mp, pl.ds(start, tk), :], qa_ref[mp, :, first * tq:],
                        preferred_element_type=F32) for mp in range(2)]

    def fold(j, first, s, extra, tile_bias):
        cols = slice(first * tq, group * tq)
        vt = vt_ref[j]
        if extra is not None:
            s = [jnp.concatenate([x[:, :tq] + extra, x[:, tq:]], axis=1) if x.shape[1] > tq
                 else x + extra for x in s]
        for mp in range(2):
            m_old = m_ref[mp, :, cols]
            m_new = jnp.maximum(m_old, jnp.max(s[mp], axis=0, keepdims=True) + tile_bias)
            p = jnp.exp2(s[mp] - (m_new - tile_bias))
            pv = jnp.dot(vt, p.astype(BF16), preferred_element_type=F32)
            alpha = jnp.exp2(m_old - m_new)
            l_ref[mp, :, cols] = alpha * l_ref[mp, :, cols] + jnp.sum(p, axis=0, keepdims=True)
            acc_ref[mp, :, cols] = alpha * acc_ref[mp, :, cols] + pv
            m_ref[mp, :, cols] = m_new

    def put(ref, s):
        ref[0], ref[1] = s[0], s[1]

    def past_bias(j):
        return slope * jnp.full((1, 1), (j - gi * group) * tk, jnp.int32).astype(F32) + lane_offset

    put(sa_ref, scores(0, 0))

    def past_pair(jj, carry):
        j = 2 * jj
        put(sb_ref, scores(j + 1, 0))
        fold(j, 0, [sa_ref[0], sa_ref[1]], None, past_bias(j))
        put(sa_ref, scores(j + 2, 0))
        fold(j + 1, 0, [sb_ref[0], sb_ref[1]], None, past_bias(j + 1))
        return carry

    lax.fori_loop(0, gi * group // 2, past_pair, 0)

    c = lax.broadcasted_iota(jnp.int32, (tk, tq), 0)
    r = lax.broadcasted_iota(jnp.int32, (tk, tq), 1)
    ahead = jnp.where((c // CHUNK) == (r // CHUNK), (2.0 * slope) * (r - c).astype(F32), -jnp.inf)
    diag_extra = jnp.where(c <= r, 0.0, ahead)
    s_cur = [sa_ref[0], sa_ref[1]]
    for b in range(group):
        s_next = scores(gi * group + b + 1, b + 1) if b + 1 < group else None
        width = (group - b) * tq
        fold(gi * group + b, b, s_cur, diag_extra, lane_offset[:, :width])
        s_cur = s_next

    lam = (jnp.exp(jnp.sum(lq1_ref[...] * lk1_ref[...], axis=-1, keepdims=True))
           - jnp.exp(jnp.sum(lq2_ref[...] * lk2_ref[...], axis=-1, keepdims=True)) + lambda_init)
    ot = acc_ref[0] / l_ref[0] - lam * (acc_ref[1] / l_ref[1])
    o_ref[...] = (_rms(ot.T, sub_ref[...]) * (1.0 - lambda_init)).astype(o_ref.dtype)


def diff_attention(proj, q_norm, k_norm, lq1, lk1, lq2, lk2, sub_norm, lambda_init, *, batch,
                   out_dtype):
    t = proj.shape[0]
    seq = t // batch
    tq = min(DIFF_TILE, seq)
    tg = min(DIFF_GROUP * tq, seq)
    ng = seq // tg
    hw = DIFF_V_DIM
    assert tq & (tq - 1) == 0 and tq <= 256
    assert ng == 1 or (tg // tq) % 2 == 0
    slope = LOG2E * jnp.exp2(-8.0 * jnp.arange(1, DIFF_HEADS + 1, dtype=F32) / DIFF_HEADS)
    parts, rest = [], slope
    for _ in range(N_SLOPE_PARTS):
        parts.append(rest.astype(BF16).astype(F32))
        rest = rest - parts[-1]
    slopes = jnp.stack(parts + [slope], axis=1)
    vec = lambda a: a.reshape(1, -1).astype(F32)
    two = lambda a: jnp.tile(a.reshape(1, -1).astype(F32), (1, 2))
    small = lambda n: pl.BlockSpec((1, n), lambda b, h, i: (0, 0))
    return pl.pallas_call(
        functools.partial(_diff_attn_body, lambda_init=lambda_init, tq=tq),
        grid=(batch, DIFF_HEADS, ng),
        in_specs=[pl.BlockSpec(memory_space=pltpu.SMEM),
                  pl.BlockSpec((tg, hw), lambda b, h, i: (b * ng + i, h)),
                  pl.BlockSpec((seq, hw), lambda b, h, i: (b, DIFF_HEADS + h)),
                  pl.BlockSpec((seq, hw), lambda b, h, i: (b, 2 * DIFF_HEADS + h)),
                  small(hw), small(hw),
                  small(DIFF_HEAD_DIM), small(DIFF_HEAD_DIM), small(DIFF_HEAD_DIM), small(DIFF_HEAD_DIM),
                  small(hw)],
        out_specs=pl.BlockSpec((tg, hw), lambda b, h, i: (b * ng + i, h)),
        out_shape=jax.ShapeDtypeStruct((t, DIFF_HEADS * hw), out_dtype),
        scratch_shapes=[pltpu.VMEM((2, seq, hw), BF16), pltpu.VMEM((seq // tq, hw, tq), BF16),
                        pltpu.VMEM((2, hw, tg), BF16),
                        pltpu.VMEM((2, tq, tg), F32), pltpu.VMEM((2, tq, tg), F32),
                        pltpu.VMEM((2, 1, tg), F32), pltpu.VMEM((2, 1, tg), F32),
                        pltpu.VMEM((2, hw, tg), F32)],
        compiler_params=_params("parallel", "parallel", "arbitrary"),
        name="diff_attention",
    )(slopes, proj, proj, proj, two(q_norm), two(k_norm), vec(lq1), vec(lk1), vec(lq2), vec(lk2),
      vec(sub_norm))


def _silu(x):
    h = 0.5 * x
    return h + h * jnp.tanh(h)


def _softplus(x):
    return jnp.maximum(x, 0.0) + jnp.log1p(jnp.exp(-jnp.abs(x)))


def _ssm_body(z_ref, xs_ref, bc_ref, dt_ref, cwx_ref, cwb_ref, cbx_ref, cbb_ref, dtb_ref, alog_ref,
              dskip_ref, gn_ref, ex_ref, o_ref, xcat_ref, bcat_ref, state_ref, xc_ref, bcc_ref, e_ref):
    q = z_ref.shape[0]
    pad = 8
    first_half = lax.broadcasted_iota(jnp.int32, (1, LANES), 1) < SSM_HEAD_DIM

    @pl.when(pl.program_id(1) == 0)
    def _():
        xcat_ref[:, 0:pad, :] = jnp.zeros((xcat_ref.shape[0], pad, LANES), F32)
        bcat_ref[:, 0:pad, :] = jnp.zeros((bcat_ref.shape[0], pad, LANES), F32)
        state_ref[...] = jnp.zeros(state_ref.shape, F32)

    ph = SSM_CONV
    nph = q // ph

    def conv_silu(src_ref, cat_ref, w_ref, b_ref, dst_ref):
        for sl in range(cat_ref.shape[0]):
            cols = slice(sl * LANES, (sl + 1) * LANES)
            cat_ref[sl, pad:pad + q, :] = src_ref[:, cols].astype(F32)
            taps = {}
            for j in range(ph):
                acc = b_ref[:, cols]
                for k in range(SSM_CONV):
                    off = pad + j - (SSM_CONV - 1) + k
                    if off not in taps:
                        taps[off] = cat_ref[sl, pl.ds(off, nph, stride=ph), :]
                    acc = acc + w_ref[k:k + 1, cols] * taps[off]
                dst_ref[sl, pl.ds(j, nph, stride=ph), :] = _silu(acc)
            cat_ref[sl, 0:pad, :] = cat_ref[sl, q:q + pad, :]

    conv_silu(xs_ref, xcat_ref, cwx_ref, cbx_ref, xc_ref)
    conv_silu(bc_ref, bcat_ref, cwb_ref, cbb_ref, bcc_ref)

    dtv = _softplus(dt_ref[...] + dtb_ref[...])
    adt = dtv * (-jnp.exp(alog_ref[...]))
    row = lax.broadcasted_iota(jnp.int32, (q, q), 0)
    col = lax.broadcasted_iota(jnp.int32, (q, q), 1)
    causal = col <= row
    cs = jnp.dot(causal.astype(F32), adt, preferred_element_type=F32, precision=lax.Precision.HIGHEST)
    cs_t = cs.T
    cs_last = cs[q - 1:q, :]
    dfs = jnp.exp(cs)
    dte = jnp.exp(cs_last - cs)
    cd = jnp.exp(cs_last)

    def pieces(a):
        hi = a.astype(BF16)
        return jnp.concatenate([hi, (a - hi.astype(F32)).astype(BF16)], axis=1)

    cd8 = jnp.broadcast_to(cd, (8, LANES))
    spread = jnp.dot(jnp.concatenate([pieces(dtv), pieces(dfs), pieces(dte), pieces(cd8)], axis=0),
                     ex_ref[...], preferred_element_type=F32)
    for i in range(3):
        e_ref[i] = spread[i * q:(i + 1) * q]
    e_ref[3, 0:8, :] = spread[3 * q:3 * q + 8]

    gw = SSM_HPG * SSM_HEAD_DIM
    for g in range(SSM_GROUPS):
        gs = slice(g * gw, (g + 1) * gw)
        bg = bcc_ref[g].astype(BF16)
        cg = bcc_ref[SSM_GROUPS + g].astype(BF16)
        cb = lax.dot_general(cg, bg, (((1,), (1,)), ((), ())), preferred_element_type=F32)
        xg = jnp.concatenate([xc_ref[2 * g], xc_ref[2 * g + 1]], axis=1)
        xdt = xg * e_ref[0, :, gs]
        y = jnp.dot(cg, state_ref[g].astype(BF16), preferred_element_type=F32) * e_ref[1, :, gs]
        y = y + dskip_ref[:, gs] * xg
        yd = []
        for p2 in range(SSM_HPG // 2):
            xp = xdt[:, p2 * LANES:(p2 + 1) * LANES]
            acc = None
            for half in range(2):
                h = g * SSM_HPG + 2 * p2 + half
                decay = jnp.exp(jnp.where(causal, cs[:, h:h + 1] - cs_t[h:h + 1, :], -jnp.inf))
                m = (cb * decay).astype(BF16)
                keep = first_half if half == 0 else jnp.logical_not(first_half)
                part = jnp.dot(m, jnp.where(keep, xp, 0.0).astype(BF16), preferred_element_type=F32)
                acc = part if acc is None else acc + part
            yd.append(acc)
        y = y + jnp.concatenate(yd, axis=1)
        wgt = (xdt * e_ref[2, :, gs]).astype(BF16)
        state_ref[g] = state_ref[g] * e_ref[3, 0:1, gs] + lax.dot_general(
            bg, wgt, (((0,), (0,)), ((), ())), preferred_element_type=F32)
        yg = y * _silu(z_ref[:, gs].astype(F32))
        o_ref[:, gs] = _rms(yg, gn_ref[:, gs]).astype(o_ref.dtype)


def ssm_mixer(proj, dt_raw, conv_w, conv_b, dt_bias, a_log, d_skip, gate_norm, *, batch, out_dtype):
    t = proj.shape[0]
    seq = t // batch
    q = min(SSM_TILE, seq)
    nt = seq // q
    di = SSM_D_INNER
    pad_lanes = lambda a: jnp.pad(a.reshape(1, -1).astype(F32), ((0, 0), (0, LANES - a.shape[-1])))
    tile = lambda c: pl.BlockSpec((q, di), lambda b, i: (b * nt + i, c))
    assert SSM_STATE == LANES and proj.dtype == BF16
    head_of_col = jnp.arange(di, dtype=jnp.int32) // SSM_HEAD_DIM
    onehot = (jnp.arange(LANES, dtype=jnp.int32)[:, None] == head_of_col[None, :]).astype(BF16)
    spread = jnp.concatenate([onehot, onehot], axis=0)
    return pl.pallas_call(
        _ssm_body,
        grid=(batch, nt),
        in_specs=[tile(0), tile(1), tile(2),
                  pl.BlockSpec((q, LANES), lambda b, i: (b * nt + i, 0)),
                  _resident((SSM_CONV, di)), _resident((SSM_CONV, di)),
                  _resident((1, di)), _resident((1, di)),
                  _resident((1, LANES)), _resident((1, LANES)),
                  _resident((1, di)), _resident((1, di)), _resident((2 * LANES, di))],
        out_specs=pl.BlockSpec((q, di), lambda b, i: (b * nt + i, 0)),
        out_shape=jax.ShapeDtypeStruct((t, di), out_dtype),
        scratch_shapes=[pltpu.VMEM((di // LANES, q + 8, LANES), F32),
                        pltpu.VMEM((di // LANES, q + 8, LANES), F32),
                        pltpu.VMEM((SSM_GROUPS, SSM_STATE, SSM_HPG * SSM_HEAD_DIM), F32),
                        pltpu.VMEM((di // LANES, q, LANES), F32), pltpu.VMEM((di // LANES, q, LANES), F32),
                        pltpu.VMEM((4, q, di), F32)],
        compiler_params=_params("parallel", "arbitrary"),
        name="ssm_mixer",
    )(proj, proj, proj, dt_raw,
      conv_w[:, :di], conv_w[:, di:], conv_b[:di].reshape(1, di), conv_b[di:].reshape(1, di),
      pad_lanes(dt_bias), pad_lanes(a_log),
      jnp.repeat(d_skip.astype(F32), SSM_HEAD_DIM).reshape(1, di), gate_norm.reshape(1, di), spread)


def kernel(x, mem, norm_mix, norm_ffn, mem_norm, w_mem_kv, xq_norm, xk_norm, w_ff1, w_ff2, w_in_ssm, ssm_conv_w, ssm_conv_b, ssm_dt_bias, ssm_a_log, ssm_d, ssm_gate_norm, w_out_ssm, w_in_diff, diff_q_norm, diff_k_norm, diff_lq1, diff_lk1, diff_lq2, diff_lk2, diff_sub_norm, w_out_diff, w_in_sgu, sgu_v_norm, sgu_w_s, sgu_b_s, w_out_sgu):
    batch, seq, d = x.shape
    depth = norm_mix.shape[0]
    t = batch * seq
    act = BF16
    tm = min(512, t)
    xf = x.reshape(t, d)

    mem_f = mem.reshape(-1, d)
    w_kv_all = jnp.moveaxis(w_mem_kv, 0, 1).reshape(d, -1).astype(BF16)
    kv_all = norm_matmul(mem_f, mem_norm, w_kv_all, tm=min(256, mem_f.shape[0]), n_chunk=2 * X_WIDTH,
                         out_dtype=act)

    for i in range(depth):
        kind, j = i % N_MIXERS, i // N_MIXERS
        if kind == 0:
            w_in = w_in_ssm[j]
            n_main = 2 * SSM_D_INNER + 2 * SSM_BC
            w_main = jnp.concatenate([w_in[:, :n_main], w_in[:, -X_WIDTH:]], axis=1).astype(BF16)
            w_dt = jnp.pad(w_in[:, n_main:n_main + SSM_HEADS], ((0, 0), (0, LANES - SSM_HEADS))).astype(BF16)
            proj, dt_raw = norm_matmul(xf, norm_mix[i], w_main, w_dt, tm=tm, n_chunk=1024, out_dtype=act)
            mix = ssm_mixer(proj, dt_raw, ssm_conv_w[j], ssm_conv_b[j], ssm_dt_bias[j], ssm_a_log[j],
                            ssm_d[j], ssm_gate_norm[j], batch=batch, out_dtype=act)
            q_col = n_main // X_WIDTH
            w_out = w_out_ssm[j]
        elif kind == 1:
            proj = norm_matmul(xf, norm_mix[i], w_in_diff[j].astype(BF16), tm=tm, n_chunk=1024, out_dtype=act)
            lambda_init = 0.8 - 0.6 * math.exp(-0.3 * i)
            mix = diff_attention(proj, diff_q_norm[j], diff_k_norm[j], diff_lq1[j], diff_lk1[j],
                                 diff_lq2[j], diff_lk2[j], diff_sub_norm[j], lambda_init, batch=batch,
                                 out_dtype=act)
            q_col = 3
            w_out = w_out_diff[j]
        else:
            proj = norm_matmul(xf, norm_mix[i], w_in_sgu[j].astype(BF16), tm=tm, n_chunk=1024, out_dtype=act)
            mix = sgu_mixer(proj, sgu_v_norm[j], sgu_w_s[j], sgu_b_s[j], tm=tm, out_dtype=act)
            q_col = 2
            w_out = w_out_sgu[j]
        cross = cross_attention(proj, q_col, kv_all, i, xq_norm[i], xk_norm[i], batch=batch,
                                tq=min(512, seq), out_dtype=act)
        wm = mix.shape[1]
        xf = out_proj_ffn(xf, mix, cross, w_out[:wm].astype(BF16), w_out[wm:].astype(BF16),
                          norm_ffn[i], w_ff1[i].astype(BF16), w_ff2[i].astype(BF16), tm=tm)
    return xf.reshape(batch, seq, d)
```
